```python
import math
import jax, jax.numpy as jnp
from jax import lax
import numpy as np

D_MODEL = 2048
BATCH = 4
SEQ = 4096
DEPTH = 1

PLE_DIM = 256
ROPE_THETA = 500000.0
ROPE_FRACTION = 4
Q_BLOCK = 128
NORM_EPS = 1e-6
NEG_INF = -1e30

DIFF_HEADS = 8
DIFF_SUB_DIM = 64
DIFF_V_DIM = 2 * DIFF_SUB_DIM
NSA_HEADS = 16
NSA_KV_GROUPS = 2
NSA_GROUP_SIZE = NSA_HEADS // NSA_KV_GROUPS
NSA_HEAD_DIM = 64
CMP_BLOCK = 32
CMP_STRIDE = 16
CMP_HIDDEN = 256
SLC_BLOCK = 64
SLC_TOP_N = 16
SLC_FORCED_BONUS = 1e4
WINDOW = 512
D_FF = 4 * D_MODEL

DIFF_W = DIFF_HEADS * DIFF_V_DIM
NSA_W = NSA_HEADS * NSA_HEAD_DIM
NSA_KV_W = NSA_KV_GROUPS * NSA_HEAD_DIM
IN_SIZES = (DIFF_W, DIFF_W, DIFF_W,
            NSA_W,
            NSA_KV_W, NSA_KV_W,
            NSA_KV_W, NSA_KV_W,
            NSA_KV_W, NSA_KV_W,
            3 * NSA_HEADS,
            D_MODEL, D_MODEL)
IN_W = 3 * DIFF_W + NSA_W + 6 * NSA_KV_W + 3 * NSA_HEADS + 2 * D_MODEL

kernel_name = 'hybrid_diffattn_nsa_sqrelu_ple'


def rms_norm(x, g=None):
    xf = x.astype(jnp.float32)
    y = xf * lax.rsqrt(jnp.mean(xf * xf, axis=-1, keepdims=True) + NORM_EPS)
    if g is not None:
        y = y * g.astype(jnp.float32)
    return y.astype(x.dtype)


def rope_tables(positions, rot_dim):
    inv_freq = jnp.power(ROPE_THETA, -jnp.arange(0, rot_dim, 2, dtype=jnp.float32) / rot_dim)
    ang = positions.astype(jnp.float32)[..., None] * inv_freq
    return jnp.cos(ang), jnp.sin(ang)


def apply_partial_rope(x, cos, sin):
    half = cos.shape[-1]
    r = 2 * half
    x1 = x[..., :half].astype(jnp.float32)
    x2 = x[..., half:r].astype(jnp.float32)
    c = cos[:, :, None, :]
    s = sin[:, :, None, :]
    rot = jnp.concatenate([x1 * c - x2 * s, x2 * c + x1 * s], axis=-1).astype(x.dtype)
    return jnp.concatenate([rot, x[..., r:]], axis=-1)


def masked_softmax(s, mask, axis=-1):
    p = jax.nn.softmax(jnp.where(mask, s, NEG_INF), axis=axis)
    return p * mask


def to_blocks(a):
    b, s = a.shape[:2]
    return jnp.moveaxis(a.reshape((b, s // Q_BLOCK, Q_BLOCK) + a.shape[2:]), 1, 0)


def from_blocks(a):
    a = jnp.moveaxis(a, 0, 1)
    return a.reshape((a.shape[0], a.shape[1] * a.shape[2]) + a.shape[3:])


def split_cols(z, sizes):
    outs, off = [], 0
    for size in sizes:
        outs.append(z[..., off:off + size])
        off += size
    return outs


def diff_attention(q, k, v, lam, lambda_init, subln_g):
    b, s = q.shape[:2]
    scale = DIFF_SUB_DIM ** -0.5
    key_pos = jnp.arange(s)
    vf = v.astype(jnp.float32)

    def block(args):
        qb, bi = args
        sc = jnp.einsum('bqhcd,bkhcd->bhcqk', qb, k,
                        preferred_element_type=jnp.float32) * scale
        q_pos = bi * Q_BLOCK + jnp.arange(Q_BLOCK)
        mask = key_pos[None, :] <= q_pos[:, None]
        prob = masked_softmax(sc, mask)
        attn = prob[:, :, 0] - lam * prob[:, :, 1]
        o = jnp.einsum('bhqk,bkhd->bqhd', attn, vf)
        o = rms_norm(o, subln_g) * (1.0 - lambda_init)
        return o.astype(v.dtype)

    out = lax.map(block, (to_blocks(q), jnp.arange(s // Q_BLOCK)))
    return from_blocks(out).reshape(b, s, DIFF_W)


def compress_tokens(kv, pos_emb, w1, w2):
    b, s, g, d = kv.shape
    n_cmp = (s - CMP_BLOCK) // CMP_STRIDE + 1
    idx = np.arange(n_cmp)[:, None] * CMP_STRIDE + np.arange(CMP_BLOCK)[None, :]
    blocks = kv[:, idx] + pos_emb[:, None, :]
    blocks = blocks.transpose(0, 1, 3, 2, 4).reshape(b, n_cmp, g, CMP_BLOCK * d)
    return jax.nn.gelu(blocks @ w1) @ w2


def nsa_attention(q_rot, q_plain, kc, vc, ks, vs, kw, vw, gates):
    b, s, g, r, d = q_rot.shape
    n_cmp = kc.shape[1]
    n_sel = s // SLC_BLOCK
    top_n = min(SLC_TOP_N, n_sel)
    scale = NSA_HEAD_DIM ** -0.5
    cmp_end = jnp.arange(n_cmp) * CMP_STRIDE + CMP_BLOCK - 1
    cs = np.arange(n_cmp) * CMP_STRIDE
    ss = np.arange(n_sel) * SLC_BLOCK
    overlap = jnp.asarray(((cs[:, None] < ss[None, :] + SLC_BLOCK) &
                           (cs[:, None] + CMP_BLOCK > ss[None, :])).astype(np.float32))
    vc_f = vc.astype(jnp.float32)
    ks_blocks = ks.reshape(b, n_sel, SLC_BLOCK, g, d).transpose(0, 3, 1, 2, 4)
    vs_blocks = vs.astype(jnp.float32).reshape(b, n_sel, SLC_BLOCK, g, d).transpose(0, 3, 1, 2, 4)
    kw_pad = jnp.pad(kw, ((0, 0), (WINDOW, 0), (0, 0), (0, 0)))
    vw_pad = jnp.pad(vw.astype(jnp.float32), ((0, 0), (WINDOW, 0), (0, 0), (0, 0)))
    b_ix = jnp.arange(b)[:, None, None, None]
    g_ix = jnp.arange(g)[None, :, None, None]
    sel_j = jnp.arange(n_sel)

    def block(args):
        qr, qp, gb, bi = args
        q_pos = bi * Q_BLOCK + jnp.arange(Q_BLOCK)
        s_c = jnp.einsum('bqgrd,bngd->bgrqn', qp, kc,
                         preferred_element_type=jnp.float32) * scale
        p_c = masked_softmax(s_c, cmp_end[None, :] <= q_pos[:, None])
        o_c = jnp.einsum('bgrqn,bngd->bqgrd', p_c, vc_f)
        imp = jnp.einsum('bgrqn,nj->bgqj', p_c, overlap)
        q_blk = q_pos // SLC_BLOCK
        valid = sel_j[None, :] <= q_blk[:, None]
        forced = ((sel_j[None, :] == 0) | (sel_j[None, :] == q_blk[:, None]) |
                  (sel_j[None, :] == q_blk[:, None] - 1))
        score = jnp.where(valid, imp + SLC_FORCED_BONUS * forced, -1.0)
        _, sel = lax.top_k(score, top_n)
        k_sel = ks_blocks[b_ix, g_ix, sel]
        v_sel = vs_blocks[b_ix, g_ix, sel]
        s_s = jnp.einsum('bqgrd,bgqtkd->bgrqtk', qr, k_sel,
                         preferred_element_type=jnp.float32) * scale
        sel_pos = sel[..., None] * SLC_BLOCK + jnp.arange(SLC_BLOCK)
        m_s = (sel_pos <= q_pos[None, None, :, None, None])[:, :, None]
        p_s = masked_softmax(s_s, m_s, axis=(-2, -1))
        o_s = jnp.einsum('bgrqtk,bgqtkd->bqgrd', p_s, v_sel)
        start = bi * Q_BLOCK
        k_win = lax.dynamic_slice_in_dim(kw_pad, start, WINDOW + Q_BLOCK, axis=1)
        v_win = lax.dynamic_slice_in_dim(vw_pad, start, WINDOW + Q_BLOCK, axis=1)
        win_pos = start - WINDOW + jnp.arange(WINDOW + Q_BLOCK)
        dist = q_pos[:, None] - win_pos[None, :]
        m_w = (dist >= 0) & (dist < WINDOW) & (win_pos[None, :] >= 0)
        s_w = jnp.einsum('bqgrd,bkgd->bgrqk', qr, k_win,
                         preferred_element_type=jnp.float32) * scale
        p_w = masked_softmax(s_w, m_w)
        o_w = jnp.einsum('bgrqk,bkgd->bqgrd', p_w, v_win)
        gf = gb.astype(jnp.float32)
        o = gf[..., 0:1] * o_c + gf[..., 1:2] * o_s + gf[..., 2:3] * o_w
        return o.astype(q_rot.dtype)

    out = lax.map(block, (to_blocks(q_rot), to_blocks(q_plain), to_blocks(gates),
                          jnp.arange(s // Q_BLOCK)))
    return from_blocks(out).reshape(b, s, NSA_W)


def hybrid_layer(x, p_i, cos, sin, lambda_init, norm_mix, w_in, diff_q_norm, diff_k_norm,
                 diff_lambda, diff_subln, nsa_q_norm, nsa_k_norm, cmp_pos, cmp_w1, cmp_w2,
                 w_proj_diff, w_proj_nsa, w_out, norm_mlp, w_mlp_up, w_mlp_down,
                 w_ple_proj, norm_ple, w_ple_gate):
    b, s, _ = x.shape
    h = rms_norm(x, norm_mix)
    z = h @ w_in
    (dq, dk, dv, nq, kc_r, vc_r, ks_r, vs_r, kw_r, vw_r,
     ng, g_a, g_b) = split_cols(z, IN_SIZES)

    dq = rms_norm(dq.reshape(b, s, 2 * DIFF_HEADS, DIFF_SUB_DIM), diff_q_norm)
    dk = rms_norm(dk.reshape(b, s, 2 * DIFF_HEADS, DIFF_SUB_DIM), diff_k_norm)
    dq = apply_partial_rope(dq, cos, sin).reshape(b, s, DIFF_HEADS, 2, DIFF_SUB_DIM)
    dk = apply_partial_rope(dk, cos, sin).reshape(b, s, DIFF_HEADS, 2, DIFF_SUB_DIM)
    dv = dv.reshape(b, s, DIFF_HEADS, DIFF_V_DIM)
    lp = diff_lambda.astype(jnp.float32)
    lam = (jnp.exp(jnp.sum(lp[0] * lp[1])) - jnp.exp(jnp.sum(lp[2] * lp[3]))
           + lambda_init)
    y_a = diff_attention(dq, dk, dv, lam, lambda_init, diff_subln)

    nq = rms_norm(nq.reshape(b, s, NSA_HEADS, NSA_HEAD_DIM), nsa_q_norm)
    q_rot = apply_partial_rope(nq, cos, sin)
    grp = (b, s, NSA_KV_GROUPS, NSA_GROUP_SIZE, NSA_HEAD_DIM)
    kv_shape = (b, s, NSA_KV_GROUPS, NSA_HEAD_DIM)
    kc = rms_norm(compress_tokens(kc_r.reshape(kv_shape), cmp_pos[0], cmp_w1[0], cmp_w2[0]),
                  nsa_k_norm)
    vc = compress_tokens(vc_r.reshape(kv_shape), cmp_pos[1], cmp_w1[1], cmp_w2[1])
    ks = apply_partial_rope(rms_norm(ks_r.reshape(kv_shape), nsa_k_norm), cos, sin)
    kw = apply_partial_rope(rms_norm(kw_r.reshape(kv_shape), nsa_k_norm), cos, sin)
    gates = jax.nn.sigmoid(ng.reshape(b, s, NSA_KV_GROUPS, NSA_GROUP_SIZE, 3))
    y_b = nsa_attention(q_rot.reshape(grp), nq.reshape(grp), kc, vc, ks,
                        vs_r.reshape(kv_shape), kw, vw_r.reshape(kv_shape), gates)

    merged = (jax.nn.sigmoid(g_a) * (y_a @ w_proj_diff) +
              jax.nn.sigmoid(g_b) * (y_b @ w_proj_nsa))
    x = x + merged @ w_out

    h2 = rms_norm(x, norm_mlp)
    x = x + jnp.square(jax.nn.relu(h2 @ w_mlp_up)) @ w_mlp_down

    e = rms_norm(p_i @ w_ple_proj, norm_ple)
    x = x + jax.nn.sigmoid(rms_norm(x) @ w_ple_gate) * e
    return x


def setup_inputs(seed: int = 0) -> dict:
    key = jax.random.key(seed)
    ks = jax.random.split(key, 24)

    def nrm(k, shape, scale):
        return jax.random.normal(k, shape, dtype=jnp.float32) * scale

    def gain(k, n):
        return 1.0 + 0.1 * jax.random.normal(k, (DEPTH, n), dtype=jnp.float32)

    return {
        'x': nrm(ks[0], (BATCH, SEQ, D_MODEL), 1.0),
        'p': nrm(ks[1], (DEPTH, BATCH, SEQ, PLE_DIM), 1.0),
        'positions': jnp.broadcast_to(jnp.arange(SEQ, dtype=jnp.int32), (BATCH, SEQ)),
        'norm_mix': gain(ks[2], D_MODEL),
        'w_in': nrm(ks[3], (DEPTH, D_MODEL, IN_W), D_MODEL ** -0.5),
        'diff_q_norm': gain(ks[4], DIFF_SUB_DIM),
        'diff_k_norm': gain(ks[5], DIFF_SUB_DIM),
        'diff_lambda': nrm(ks[6], (DEPTH, 4, DIFF_SUB_DIM), 0.1),
        'diff_subln': gain(ks[7], DIFF_V_DIM),
        'nsa_q_norm': gain(ks[8], NSA_HEAD_DIM),
        'nsa_k_norm': gain(ks[9], NSA_HEAD_DIM),
        'cmp_pos': nrm(ks[10], (DEPTH, 2, CMP_BLOCK, NSA_HEAD_DIM), 0.1),
        'cmp_w1': nrm(ks[11], (DEPTH, 2, CMP_BLOCK * NSA_HEAD_DIM, CMP_HIDDEN),
                      (CMP_BLOCK * NSA_HEAD_DIM) ** -0.5),
        'cmp_w2': nrm(ks[12], (DEPTH, 2, CMP_HIDDEN, NSA_HEAD_DIM), CMP_HIDDEN ** -0.5),
        'w_proj_diff': nrm(ks[13], (DEPTH, DIFF_W, D_MODEL), DIFF_W ** -0.5),
        'w_proj_nsa': nrm(ks[14], (DEPTH, NSA_W, D_MODEL), NSA_W ** -0.5),
        'w_out': nrm(ks[15], (DEPTH, D_MODEL, D_MODEL), D_MODEL ** -0.5),
        'norm_mlp': gain(ks[16], D_MODEL),
        'w_mlp_up': nrm(ks[17], (DEPTH, D_MODEL, D_FF), D_MODEL ** -0.5),
        'w_mlp_down': nrm(ks[18], (DEPTH, D_FF, D_MODEL), D_FF ** -0.5),
        'w_ple_proj': nrm(ks[19], (DEPTH, PLE_DIM, D_MODEL), PLE_DIM ** -0.5),
        'norm_ple': gain(ks[20], D_MODEL),
        'w_ple_gate': nrm(ks[21], (DEPTH, D_MODEL, D_MODEL), D_MODEL ** -0.5),
    }


def reference(x, p, positions, norm_mix, w_in, diff_q_norm, diff_k_norm, diff_lambda,
              diff_subln, nsa_q_norm, nsa_k_norm, cmp_pos, cmp_w1, cmp_w2, w_proj_diff,
              w_proj_nsa, w_out, norm_mlp, w_mlp_up, w_mlp_down, w_ple_proj, norm_ple,
              w_ple_gate):
    cos, sin = rope_tables(positions, NSA_HEAD_DIM // ROPE_FRACTION)
    for i in range(DEPTH):
        lambda_init = 0.8 - 0.6 * math.exp(-0.3 * i)
        x = hybrid_layer(x, p[i], cos, sin, lambda_init, norm_mix[i], w_in[i],
                         diff_q_norm[i], diff_k_norm[i], diff_lambda[i], diff_subln[i],
                         nsa_q_norm[i], nsa_k_norm[i], cmp_pos[i], cmp_w1[i], cmp_w2[i],
                         w_proj_diff[i], w_proj_nsa[i], w_out[i], norm_mlp[i],
                         w_mlp_up[i], w_mlp_down[i], w_ple_proj[i], norm_ple[i],
                         w_ple_gate[i])
    return x
```

```python
import functools
import math

import jax
import jax.numpy as jnp
from jax import lax
from jax.experimental import pallas as pl
from jax.experimental.pallas import tpu as pltpu

F32 = jnp.float32
BF16 = jnp.bfloat16

D_MODEL = 2048
PLE_DIM = 256
ROPE_THETA = 500000.0
ROPE_FRACTION = 4
NORM_EPS = 1e-6
NEG_INF = -1e30
DIFF_HEADS = 8
DIFF_SUB_DIM = 64
DIFF_V_DIM = 2 * DIFF_SUB_DIM
NSA_HEADS = 16
NSA_KV_GROUPS = 2
NSA_GROUP_SIZE = NSA_HEADS // NSA_KV_GROUPS
NSA_HEAD_DIM = 64
CMP_BLOCK = 32
CMP_STRIDE = 16
CMP_HIDDEN = 256
SLC_BLOCK = 64
SLC_TOP_N = 16
SLC_FORCED_BONUS = 1e4
WINDOW = 512
D_FF = 4 * D_MODEL
DIFF_W = DIFF_HEADS * DIFF_V_DIM
NSA_W = NSA_HEADS * NSA_HEAD_DIM
NSA_KV_W = NSA_KV_GROUPS * NSA_HEAD_DIM
ROT_HALF = NSA_HEAD_DIM // ROPE_FRACTION // 2
QK_SCALE = NSA_HEAD_DIM ** -0.5

LANES = 128
VMEM_LIMIT = 56 * 1024 * 1024

Z_DQ, Z_DK, Z_DV, Z_NQ = 0, 1024, 2048, 3072
Z_KC, Z_VC, Z_KS, Z_VS, Z_KW, Z_VW = 4096, 4224, 4352, 4480, 4608, 4736
PREP_W = 4864
Z_NG = PREP_W


def _params(sem):
    return pltpu.CompilerParams(dimension_semantics=sem, vmem_limit_bytes=VMEM_LIMIT)


def _dot(a, b):
    return jnp.dot(a, b, preferred_element_type=F32)


def _dot_nt(a, b):
    return lax.dot_general(a, b, (((1,), (1,)), ((), ())), preferred_element_type=F32)


def _rms(x, eps=NORM_EPS):
    return x * lax.rsqrt(jnp.mean(x * x, axis=-1, keepdims=True) + eps)


def _split_dot(a, m_bf16, terms):
    acc = None
    rem = a
    for t in range(terms):
        piece = rem.astype(BF16)
        d = _dot(piece, m_bf16)
        acc = d if acc is None else acc + d
        if t + 1 < terms:
            rem = rem - piece.astype(F32)
    return acc


def _in_proj_kernel(x_ref, g_ref, w_ref, o_ref, h_ref):
    @pl.when(pl.program_id(1) == 0)
    def _():
        h_ref[...] = (_rms(x_ref[...]) * g_ref[...]).astype(BF16)

    o_ref[...] = _dot(h_ref[...], w_ref[...])


def _in_proj(x2d, g, w, tm, tn):
    n, d = x2d.shape
    nout = w.shape[1]
    return pl.pallas_call(
        _in_proj_kernel,
        out_shape=jax.ShapeDtypeStruct((n, nout), F32),
        grid=(n // tm, nout // tn),
        in_specs=[
            pl.BlockSpec((tm, d), lambda i, j: (i, 0)),
            pl.BlockSpec((1, d), lambda i, j: (0, 0)),
            pl.BlockSpec((d, tn), lambda i, j: (0, j)),
        ],
        out_specs=pl.BlockSpec((tm, tn), lambda i, j: (i, j)),
        scratch_shapes=[pltpu.VMEM((tm, d), BF16)],
        compiler_params=_params(("parallel", "arbitrary")),
        name="in_proj",
    )(x2d, g, w)


def _prep_kernel(z_ref, ra_ref, rb_ref, rc_ref, gain_ref, seg_ref,
                 qd_ref, kd_ref, vd_ref, nqp_ref, nqr_ref, ks_ref, vs_ref, kw_ref, vw_ref):
    ra, rb, rc = ra_ref[...], rb_ref[...], rc_ref[...]
    seg = seg_ref[...]
    tm = ra.shape[0]
    lane = lax.broadcasted_iota(jnp.int32, (tm, LANES), 1)
    lo_half = lane < NSA_HEAD_DIM

    def rope(y):
        return (y * ra + pltpu.roll(y, LANES - ROT_HALF, 1) * rb + pltpu.roll(y, ROT_HALF, 1) * rc)

    def head_norm(xb, gain):
        ss = _split_dot(xb * xb, seg, 2)
        return xb * lax.rsqrt(ss * (1.0 / NSA_HEAD_DIM) + NORM_EPS) * gain

    def zblk(col):
        return z_ref[:, col:col + LANES]

    for hb in range(DIFF_W // LANES):
        c = hb * LANES
        qd_ref[:, c:c + LANES] = (rope(head_norm(zblk(Z_DQ + c), gain_ref[0:1, :])) * QK_SCALE).astype(BF16)
        kd_ref[:, c:c + LANES] = rope(head_norm(zblk(Z_DK + c), gain_ref[1:2, :])).astype(BF16)
        vd_ref[:, c:c + LANES] = zblk(Z_DV + c).astype(BF16)
        yq = head_norm(zblk(Z_NQ + c), gain_ref[2:3, :])
        nqp_ref[:, c:c + LANES] = (yq * QK_SCALE).astype(BF16)
        nqr_ref[:, c:c + LANES] = (rope(yq) * QK_SCALE).astype(BF16)

    def dup(xb):
        sw = pltpu.roll(xb, NSA_HEAD_DIM, 1)
        return jnp.where(lo_half, xb, sw), jnp.where(lo_half, sw, xb)

    for col, k_out in ((Z_KS, ks_ref), (Z_KW, kw_ref)):
        for g, xg in enumerate(dup(zblk(col))):
            k_out[g] = rope(_rms(xg) * gain_ref[3:4, :]).astype(BF16)
    for col, v_out in ((Z_VS, vs_ref), (Z_VW, vw_ref)):
        for g, xg in enumerate(dup(zblk(col))):
            v_out[g] = xg.astype(BF16)


def _prep(z, ra, rb, rc, gains, seg, tm):
    n = z.shape[0]
    row = lambda i: (i, 0)
    wide = jax.ShapeDtypeStruct((n, DIFF_W), BF16)
    grp = jax.ShapeDtypeStruct((NSA_KV_GROUPS, n, LANES), BF16)
    wide_spec = pl.BlockSpec((tm, DIFF_W), row)
    grp_spec = pl.BlockSpec((NSA_KV_GROUPS, tm, LANES), lambda i: (0, i, 0))
    return pl.pallas_call(
        _prep_kernel,
        out_shape=[wide] * 5 + [grp] * 4,
        grid=(n // tm,),
        in_specs=[
            pl.BlockSpec((tm, PREP_W), row),
            pl.BlockSpec((tm, LANES), row),
            pl.BlockSpec((tm, LANES), row),
            pl.BlockSpec((tm, LANES), row),
            pl.BlockSpec((8, LANES), lambda i: (0, 0)),
            pl.BlockSpec((LANES, LANES), lambda i: (0, 0)),
        ],
        out_specs=[wide_spec] * 5 + [grp_spec] * 4,
        compiler_params=_params(("parallel",)),
        name="prep",
    )(z, ra, rb, rc, gains, seg)


def _compress_kernel(x_ref, pos_ref, w1_ref, w2_ref, gain_ref, o_ref):
    x = x_ref[...]
    nc = x.shape[0]
    a = (x + pos_ref[0:1, :]).astype(BF16)
    b = (x + pos_ref[1:2, :]).astype(BF16)
    u = _dot(a, w1_ref[0])
    v = _dot(b, w1_ref[1])
    pre = u + pltpu.roll(v, nc - 1, 0)
    cdf = 0.5 * (1.0 + jnp.tanh(math.sqrt(2.0 / math.pi) * (pre + 0.044715 * (pre * pre * pre))))
    c = _dot((pre * cdf).astype(BF16), w2_ref[...])
    is_key = pl.program_id(0) == 0
    o_ref[...] = jnp.where(is_key, _rms(c) * gain_ref[...], c).astype(BF16)


def _compress(xc, pos, w1, w2d, gain):
    _, b, g, nc, cw = xc.shape
    return pl.pallas_call(
        _compress_kernel,
        out_shape=jax.ShapeDtypeStruct((2, b, g, nc, LANES), BF16),
        grid=(2, b, g),
        in_specs=[
            pl.BlockSpec((None, None, None, nc, cw), lambda t, i, j: (t, i, j, 0, 0)),
            pl.BlockSpec((None, 2, cw), lambda t, i, j: (t, 0, 0)),
            pl.BlockSpec((None, 2, cw, CMP_HIDDEN), lambda t, i, j: (t, 0, 0, 0)),
            pl.BlockSpec((None, CMP_HIDDEN, LANES), lambda t, i, j: (t, 0, 0)),
            pl.BlockSpec((1, LANES), lambda t, i, j: (0, 0)),
        ],
        out_specs=pl.BlockSpec((None, None, None, nc, LANES), lambda t, i, j: (t, i, j, 0, 0)),
        compiler_params=_params(("parallel", "parallel", "parallel")),
        name="compress",
    )(xc, pos, w1, w2d, gain)


def _online_update(s, v, m_ref, l_ref, acc_ref):
    m_old = m_ref[...]
    m_new = jnp.maximum(m_old, jnp.max(s, axis=-1, keepdims=True))
    alpha = jnp.exp(m_old - m_new)
    p = jnp.exp(s - m_new)
    l_ref[...] = alpha * l_ref[...] + jnp.sum(p, axis=-1, keepdims=True)
    acc_ref[...] = alpha * acc_ref[...] + _dot(p.astype(BF16), v)
    m_ref[...] = m_new


def _diff_attn_kernel(lam_ref, q_ref, k_ref, v_ref, g_ref, o_ref,
                      m1, l1, a1, m2, l2, a2, *, lambda_init, t):
    qi = pl.program_id(2)
    q = q_ref[...]
    lane = lax.broadcasted_iota(jnp.int32, q.shape, 1)
    zero = jnp.zeros_like(q)
    subs = ((jnp.where(lane < DIFF_SUB_DIM, q, zero), m1, l1, a1),
            (jnp.where(lane >= DIFF_SUB_DIM, q, zero), m2, l2, a2))
    for _, m_ref, l_ref, a_ref in subs:
        m_ref[...] = jnp.full(m_ref.shape, NEG_INF, F32)
        l_ref[...] = jnp.zeros(l_ref.shape, F32)
        a_ref[...] = jnp.zeros(a_ref.shape, F32)

    def step(kt, causal):
        off = pl.multiple_of(kt * t, t)
        k = k_ref[pl.ds(off, t), :]
        v = v_ref[pl.ds(off, t), :]
        for qs, m_ref, l_ref, a_ref in subs:
            s = _dot_nt(qs, k)
            if causal:
                row = lax.broadcasted_iota(jnp.int32, s.shape, 0)
                col = lax.broadcasted_iota(jnp.int32, s.shape, 1)
                s = jnp.where(col <= row, s, NEG_INF)
            _online_update(s, v, m_ref, l_ref, a_ref)

    def body(kt, carry):
        step(kt, False)
        return carry

    lax.fori_loop(0, qi, body, 0)
    step(qi, True)

    lp = lam_ref[...]
    lam = (jnp.exp(jnp.sum(lp[0:1] * lp[1:2], axis=-1, keepdims=True))
           - jnp.exp(jnp.sum(lp[2:3] * lp[3:4], axis=-1, keepdims=True)) + lambda_init)
    o = a1[...] / l1[...] - lam * (a2[...] / l2[...])
    o_ref[...] = (_rms(o) * g_ref[...] * (1.0 - lambda_init)).astype(BF16)


def _diff_attn(lam_p, qd, kd, vd, subln, lambda_init, t):
    b, s, _ = qd.shape
    kernel = functools.partial(_diff_attn_kernel, lambda_init=lambda_init, t=t)
    kv_spec = pl.BlockSpec((None, s, LANES), lambda i, h, j: (i, 0, h))
    q_spec = pl.BlockSpec((None, t, LANES), lambda i, h, j: (i, j, h))
    stat = pltpu.VMEM((t, 1), F32)
    acc = pltpu.VMEM((t, LANES), F32)
    return pl.pallas_call(
        kernel,
        out_shape=jax.ShapeDtypeStruct((b, s, DIFF_W), BF16),
        grid=(b, DIFF_HEADS, s // t),
        in_specs=[
            pl.BlockSpec((4, DIFF_SUB_DIM), lambda i, h, j: (0, 0)),
            q_spec, kv_spec, kv_spec,
            pl.BlockSpec((1, LANES), lambda i, h, j: (0, 0)),
        ],
        out_specs=q_spec,
        scratch_shapes=[stat, stat, acc, stat, stat, acc],
        compiler_params=_params(("parallel", "parallel", "arbitrary")),
        name="diff_attn",
    )(lam_p, qd, kd, vd, subln)


def _nsa_kernel(qp_ref, qr_ref, kc_ref, vc_ref, ks_ref, vs_ref, kw_ref, vw_ref, ng_ref,
                ovl_ref, exp_ref, o_ref, m_ref, l_ref, acc_ref, *, tq, tk, n_sel, top_n, wlen):
    r_heads = NSA_GROUP_SIZE
    rows = r_heads * tq
    qi = pl.program_id(2)
    q0 = qi * tq
    lane = lax.broadcasted_iota(jnp.int32, (tq, LANES), 1)
    lo_half = lane < NSA_HEAD_DIM
    qpos = q0 + lax.broadcasted_iota(jnp.int32, (tq, 1), 0)

    def stack(x):
        parts = []
        for r in range(r_heads):
            blk = x[:, LANES * (r // 2):LANES * (r // 2 + 1)]
            keep = lo_half if r % 2 == 0 else jnp.logical_not(lo_half)
            parts.append(jnp.where(keep, blk, jnp.zeros_like(blk)))
        return jnp.concatenate(parts, axis=0)

    def per_head(mask2d, s):
        n = s.shape[-1]
        return jnp.where(mask2d[None], s.reshape(r_heads, tq, n), NEG_INF).reshape(rows, n)

    q_plain = stack(qp_ref[...])
    q_rot = stack(qr_ref[...])

    ncp = kc_ref.shape[0]
    s_c = _dot_nt(q_plain, kc_ref[...])
    n_idx = lax.broadcasted_iota(jnp.int32, (tq, ncp), 1)
    c_ok = (n_idx * CMP_STRIDE + (CMP_BLOCK - 1)) <= qpos
    s_c = per_head(c_ok, s_c)
    e_c = jnp.exp(s_c - jnp.max(s_c, axis=-1, keepdims=True))
    p_c = e_c / jnp.sum(e_c, axis=-1, keepdims=True)
    p_c = (p_c.reshape(r_heads, tq, ncp) * c_ok.astype(F32)[None])
    o_c = _dot(p_c.reshape(rows, ncp).astype(BF16), vc_ref[...])

    imp = _split_dot(jnp.sum(p_c, axis=0), ovl_ref[...], 3)
    q_blk = qpos // SLC_BLOCK
    valid = lane <= q_blk
    forced = (lane == 0) | (lane == q_blk) | (lane == q_blk - 1)
    score = jnp.where(valid, imp + SLC_FORCED_BONUS * forced.astype(F32), -1.0)
    rank = jnp.zeros((tq, LANES), F32)
    for jp in range(n_sel):
        col = score[:, jp:jp + 1]
        gt = jnp.where(col > score, 1.0, 0.0)
        ge = jnp.where(col >= score, 1.0, 0.0)
        rank = rank + jnp.where(lane > jp, ge, gt)
    sel = jnp.where(valid & (rank < float(top_n)), 1.0, 0.0).astype(BF16)

    m_ref[...] = jnp.full(m_ref.shape, NEG_INF, F32)
    l_ref[...] = jnp.zeros(l_ref.shape, F32)
    acc_ref[...] = jnp.zeros(acc_ref.shape, F32)

    def sel_step(kt, causal):
        off = pl.multiple_of(kt * tk, tk)
        k = ks_ref[pl.ds(off, tk), :]
        v = vs_ref[pl.ds(off, tk), :]
        ok = _dot(sel, exp_ref[kt]) > 0.5
        if causal:
            kpos = off + lax.broadcasted_iota(jnp.int32, (tq, tk), 1)
            ok = ok & (kpos <= qpos)
        _online_update(per_head(ok, _dot_nt(q_rot, k)), v, m_ref, l_ref, acc_ref)

    def sel_body(kt, carry):
        sel_step(kt, False)
        return carry

    kt_last = q0 // tk
    lax.fori_loop(0, kt_last, sel_body, 0)
    sel_step(kt_last, True)
    o_s = acc_ref[...] / l_ref[...]

    start = pl.multiple_of(jnp.maximum(q0 + tq - wlen, 0), tq)
    kw = kw_ref[pl.ds(start, wlen), :]
    vw = vw_ref[pl.ds(start, wlen), :]
    dist = qpos - (start + lax.broadcasted_iota(jnp.int32, (tq, wlen), 1))
    s_w = per_head((dist >= 0) & (dist < WINDOW), _dot_nt(q_rot, kw))
    e_w = jnp.exp(s_w - jnp.max(s_w, axis=-1, keepdims=True))
    o_w = _dot(e_w.astype(BF16), vw) / jnp.sum(e_w, axis=-1, keepdims=True)

    gates = jax.nn.sigmoid(ng_ref[...])

    def gate(c):
        return jnp.concatenate([gates[:, 3 * r + c:3 * r + c + 1] for r in range(r_heads)], axis=0)

    o = gate(0) * o_c + gate(1) * o_s + gate(2) * o_w
    o_ref[...] = jnp.concatenate(
        [jnp.where(lo_half, o[(2 * j) * tq:(2 * j + 1) * tq], o[(2 * j + 1) * tq:(2 * j + 2) * tq])
         for j in range(r_heads // 2)], axis=1).astype(BF16)


def _nsa_attn(nqp, nqr, cmp_kv, ks, vs, kw, vw, z, ovl, expand, tq, tk):
    b, s, _ = nqp.shape
    n_sel = s // SLC_BLOCK
    top_n = min(SLC_TOP_N, n_sel)
    wlen = min(WINDOW + tq, s)
    ncp = cmp_kv.shape[3]
    gw = NSA_GROUP_SIZE * NSA_HEAD_DIM
    nq = s // tq
    kernel = functools.partial(_nsa_kernel, tq=tq, tk=tk, n_sel=n_sel, top_n=top_n, wlen=wlen)
    q_spec = pl.BlockSpec((None, tq, gw), lambda i, g, j: (i, j, g))
    kv_spec = pl.BlockSpec((None, s, LANES), lambda i, g, j: (g, i, 0))
    rows = NSA_GROUP_SIZE * tq
    return pl.pallas_call(
        kernel,
        out_shape=jax.ShapeDtypeStruct((b, s, NSA_W), BF16),
        grid=(b, NSA_KV_GROUPS, nq),
        in_specs=[
            q_spec, q_spec,
            pl.BlockSpec((None, None, None, ncp, LANES), lambda i, g, j: (0, i, g, 0, 0)),
            pl.BlockSpec((None, None, None, ncp, LANES), lambda i, g, j: (1, i, g, 0, 0)),
            kv_spec, kv_spec, kv_spec, kv_spec,
            pl.BlockSpec((tq, LANES), lambda i, g, j: (i * nq + j, Z_NG // LANES + g)),
            pl.BlockSpec((ncp, LANES), lambda i, g, j: (0, 0)),
            pl.BlockSpec((s // tk, LANES, tk), lambda i, g, j: (0, 0, 0)),
        ],
        out_specs=q_spec,
        scratch_shapes=[pltpu.VMEM((rows, 1), F32), pltpu.VMEM((rows, 1), F32),
                        pltpu.VMEM((rows, LANES), F32)],
        compiler_params=_params(("parallel", "parallel", "arbitrary")),
        name="nsa_attn",
    )(nqp, nqr, cmp_kv, cmp_kv, ks, vs, kw, vw, z, ovl, expand)


def _merge_out_kernel(x_ref, ya_ref, yb_ref, ga_ref, gb_ref, wa_ref, wb_ref, wo_ref, o_ref, mg_ref,
                      *, tn):
    d = x_ref.shape[1]
    ya = ya_ref[...]
    yb = yb_ref[...]
    for c in range(0, d, tn):
        ta = _dot(ya, wa_ref[:, c:c + tn])
        tb = _dot(yb, wb_ref[:, c:c + tn])
        mg_ref[:, c:c + tn] = (jax.nn.sigmoid(ga_ref[:, c:c + tn]) * ta
                               + jax.nn.sigmoid(gb_ref[:, c:c + tn]) * tb).astype(BF16)
    mg = mg_ref[...]
    for c in range(0, d, tn):
        o_ref[:, c:c + tn] = x_ref[:, c:c + tn] + _dot(mg, wo_ref[:, c:c + tn])


def _merge_out(x2d, ya, yb, zg, wa, wb, wo, tm, tn):
    n, d = x2d.shape
    row = lambda i: (i, 0)
    const = lambda i: (0, 0)
    resident = pl.Buffered(1)
    return pl.pallas_call(
        functools.partial(_merge_out_kernel, tn=tn),
        out_shape=jax.ShapeDtypeStruct((n, d), F32),
        grid=(n // tm,),
        in_specs=[
            pl.BlockSpec((tm, d), row),
            pl.BlockSpec((tm, DIFF_W), row),
            pl.BlockSpec((tm, NSA_W), row),
            pl.BlockSpec((tm, d), lambda i: (i, 0)),
            pl.BlockSpec((tm, d), lambda i: (i, 1)),
            pl.BlockSpec((DIFF_W, d), const, pipeline_mode=resident),
            pl.BlockSpec((NSA_W, d), const, pipeline_mode=resident),
            pl.BlockSpec((d, d), const, pipeline_mode=resident),
        ],
        out_specs=pl.BlockSpec((tm, d), row),
        scratch_shapes=[pltpu.VMEM((tm, d), BF16)],
        compiler_params=_params(("parallel",)),
        name="merge_out",
    )(x2d, ya, yb, zg, zg, wa, wb, wo)


def _mlp_kernel(x_ref, g_ref, wu_ref, wd_ref, o_ref, h_ref, acc_ref):
    f = pl.program_id(1)

    @pl.when(f == 0)
    def _():
        h_ref[...] = (_rms(x_ref[...]) * g_ref[...]).astype(BF16)
        acc_ref[...] = jnp.zeros(acc_ref.shape, F32)

    u = jnp.maximum(_dot(h_ref[...], wu_ref[...]), 0.0)
    acc_ref[...] += _dot((u * u).astype(BF16), wd_ref[...])

    @pl.when(f == pl.num_programs(1) - 1)
    def _():
        o_ref[...] = x_ref[...] + acc_ref[...]


def _mlp(x2d, g, wu, wd, tm, tf):
    n, d = x2d.shape
    ff = wu.shape[1]
    return pl.pallas_call(
        _mlp_kernel,
        out_shape=jax.ShapeDtypeStruct((n, d), F32),
        grid=(n // tm, ff // tf),
        in_specs=[
            pl.BlockSpec((tm, d), lambda i, f: (i, 0)),
            pl.BlockSpec((1, d), lambda i, f: (0, 0)),
            pl.BlockSpec((d, tf), lambda i, f: (0, f)),
            pl.BlockSpec((tf, d), lambda i, f: (f, 0)),
        ],
        out_specs=pl.BlockSpec((tm, d), lambda i, f: (i, 0)),
        scratch_shapes=[pltpu.VMEM((tm, d), BF16), pltpu.VMEM((tm, d), F32)],
        compiler_params=_params(("parallel", "arbitrary")),
        name="mlp",
    )(x2d, g, wu, wd)


def _ple_kernel(x_ref, p_ref, g_ref, wp_ref, wg_ref, o_ref):
    x = x_ref[...]
    e = _rms(_dot(p_ref[...].astype(BF16), wp_ref[...])) * g_ref[...]
    gate = jax.nn.sigmoid(_dot(_rms(x).astype(BF16), wg_ref[...]))
    o_ref[...] = x + gate * e


def _ple(x2d, p2d, g, wp, wg, tm):
    n, d = x2d.shape
    row = lambda i: (i, 0)
    const = lambda i: (0, 0)
    resident = pl.Buffered(1)
    return pl.pallas_call(
        _ple_kernel,
        out_shape=jax.ShapeDtypeStruct((n, d), F32),
        grid=(n // tm,),
        in_specs=[
            pl.BlockSpec((tm, d), row),
            pl.BlockSpec((tm, PLE_DIM), row),
            pl.BlockSpec((1, d), const),
            pl.BlockSpec((PLE_DIM, d), const, pipeline_mode=resident),
            pl.BlockSpec((d, d), const, pipeline_mode=resident),
        ],
        out_specs=pl.BlockSpec((tm, d), row),
        compiler_params=_params(("parallel",)),
        name="ple",
    )(x2d, p2d, g, wp, wg)


def _rope_tables(positions):
    rot = 2 * ROT_HALF
    inv_freq = jnp.power(ROPE_THETA, -jnp.arange(0, rot, 2, dtype=F32) / rot)
    ang = positions.astype(F32)[..., None] * inv_freq
    cos, sin = jnp.cos(ang), jnp.sin(ang)
    n = cos.shape[0] * cos.shape[1]
    cos, sin = cos.reshape(n, ROT_HALF), sin.reshape(n, ROT_HALF)
    rest = NSA_HEAD_DIM - rot
    ones, zeros_r, zeros_h = jnp.ones((n, rest), F32), jnp.zeros((n, rest), F32), jnp.zeros((n, ROT_HALF), F32)
    ra = jnp.concatenate([cos, cos, ones], axis=1)
    rb = jnp.concatenate([-sin, zeros_h, zeros_r], axis=1)
    rc = jnp.concatenate([zeros_h, sin, zeros_r], axis=1)
    return tuple(jnp.tile(t, (1, LANES // NSA_HEAD_DIM)) for t in (ra, rb, rc))


def _regroup_w_in(w_in):
    d = w_in.shape[0]
    ng0 = 3 * DIFF_W + NSA_W + 6 * NSA_KV_W
    per_g = 3 * NSA_GROUP_SIZE
    parts = [w_in[:, :ng0]]
    for g in range(NSA_KV_GROUPS):
        parts += [w_in[:, ng0 + g * per_g:ng0 + (g + 1) * per_g], jnp.zeros((d, LANES - per_g), w_in.dtype)]
    w_main = jnp.concatenate(parts, axis=1).astype(BF16)
    w_gate = w_in[:, ng0 + 3 * NSA_HEADS:].astype(BF16)
    return w_main, w_gate


def _pick(n, pref):
    t = min(pref, n)
    while n % t:
        t //= 2
    return t


def kernel(x, p, positions, norm_mix, w_in, diff_q_norm, diff_k_norm, diff_lambda, diff_subln,
           nsa_q_norm, nsa_k_norm, cmp_pos, cmp_w1, cmp_w2, w_proj_diff, w_proj_nsa, w_out,
           norm_mlp, w_mlp_up, w_mlp_down, w_ple_proj, norm_ple, w_ple_gate):
    b, s, d = x.shape
    n = b * s
    assert d == D_MODEL and s % 512 == 0
    ra, rb, rc = _rope_tables(positions)
    seg = (jnp.arange(LANES)[:, None] // NSA_HEAD_DIM == jnp.arange(LANES)[None, :] // NSA_HEAD_DIM).astype(BF16)
    ncp = s // CMP_STRIDE
    cs = jnp.arange(ncp)[:, None] * CMP_STRIDE
    ss = jnp.arange(LANES)[None, :] * SLC_BLOCK
    ovl = ((cs < ss + SLC_BLOCK) & (cs + CMP_BLOCK > ss) & (jnp.arange(LANES)[None, :] < s // SLC_BLOCK)).astype(BF16)
    tq_nsa, tk_nsa, t_diff = 128, 512, 512
    key_blk = (jnp.arange(s) // SLC_BLOCK).reshape(s // tk_nsa, 1, tk_nsa)
    expand = (key_blk == jnp.arange(LANES)[None, :, None]).astype(BF16)

    x2d = x.reshape(n, d)
    for i in range(p.shape[0]):
        lambda_init = 0.8 - 0.6 * math.exp(-0.3 * i)
        dup = lambda v: jnp.tile(v, LANES // NSA_HEAD_DIM)
        gains = jnp.zeros((8, LANES), F32).at[0].set(dup(diff_q_norm[i])).at[1].set(dup(diff_k_norm[i]))
        gains = gains.at[2].set(dup(nsa_q_norm[i])).at[3].set(dup(nsa_k_norm[i]))

        w_main, w_gate = _regroup_w_in(w_in[i])
        z = _in_proj(x2d, norm_mix[i][None], w_main, _pick(n, 512), 512)
        zg = _in_proj(x2d, norm_mix[i][None], w_gate, _pick(n, 512), 512)
        qd, kd, vd, nqp, nqr, ks, vs, kw, vw = _prep(z, ra, rb, rc, gains, seg, _pick(n, 256))

        def chunks(col):
            a = z[:, col:col + NSA_KV_W].reshape(b, ncp, CMP_STRIDE, NSA_KV_GROUPS, NSA_HEAD_DIM)
            return a.transpose(0, 3, 1, 2, 4).reshape(b, NSA_KV_GROUPS, ncp, CMP_STRIDE * NSA_HEAD_DIM)

        half = CMP_STRIDE * NSA_HEAD_DIM
        cmp_kv = _compress(
            jnp.stack([chunks(Z_KC), chunks(Z_VC)]),
            cmp_pos[i].reshape(2, 2, half),
            cmp_w1[i].reshape(2, 2, half, CMP_HIDDEN).astype(BF16),
            jnp.tile(cmp_w2[i], (1, 1, LANES // NSA_HEAD_DIM)).astype(BF16),
            gains[3:4])

        shp = (b, s, DIFF_W)
        ya = _diff_attn(diff_lambda[i], qd.reshape(shp), kd.reshape(shp), vd.reshape(shp),
                        diff_subln[i][None], lambda_init, t_diff)
        yb = _nsa_attn(nqp.reshape(shp), nqr.reshape(shp), cmp_kv, ks, vs, kw, vw, z, ovl, expand,
                       tq_nsa, tk_nsa)

        x2d = _merge_out(x2d, ya.reshape(n, DIFF_W), yb.reshape(n, NSA_W), zg,
                         w_proj_diff[i].astype(BF16), w_proj_nsa[i].astype(BF16), w_out[i].astype(BF16),
                         _pick(n, 256), 512)
        x2d = _mlp(x2d, norm_mlp[i][None], w_mlp_up[i].astype(BF16), w_mlp_down[i].astype(BF16),
                   _pick(n, 512), 512)
        x2d = _ple(x2d, p[i].reshape(n, PLE_DIM), norm_ple[i][None], w_ple_proj[i].astype(BF16),
                   w_ple_gate[i].astype(BF16), _pick(n, 256))
    return x2d.reshape(b, s, d)
```

```python
import functools
import math

import jax
import jax.numpy as jnp
from jax import lax
from jax.experimental import pallas as pl
from jax.experimental.pallas import tpu as pltpu

F32 = jnp.float32
BF16 = jnp.bfloat16

D_MODEL = 2048
PLE_DIM = 256
ROPE_THETA = 500000.0
ROPE_FRACTION = 4
NORM_EPS = 1e-6
NEG_INF = -1e30
DIFF_HEADS = 8
DIFF_SUB_DIM = 64
DIFF_V_DIM = 2 * DIFF_SUB_DIM
NSA_HEADS = 16
NSA_KV_GROUPS = 2
NSA_GROUP_SIZE = NSA_HEADS // NSA_KV_GROUPS
NSA_HEAD_DIM = 64
CMP_BLOCK = 32
CMP_STRIDE = 16
CMP_HIDDEN = 256
SLC_BLOCK = 64
SLC_TOP_N = 16
SLC_FORCED_BONUS = 1e4
WINDOW = 512
D_FF = 4 * D_MODEL
DIFF_W = DIFF_HEADS * DIFF_V_DIM
NSA_W = NSA_HEADS * NSA_HEAD_DIM
NSA_KV_W = NSA_KV_GROUPS * NSA_HEAD_DIM
ROT_HALF = NSA_HEAD_DIM // ROPE_FRACTION // 2
QK_SCALE = NSA_HEAD_DIM ** -0.5

LANES = 128
VMEM_LIMIT = 56 * 1024 * 1024

Z_DQ, Z_DK, Z_DV, Z_NQ = 0, 1024, 2048, 3072
Z_KC, Z_VC, Z_KS, Z_VS, Z_KW, Z_VW = 4096, 4224, 4352, 4480, 4608, 4736
PREP_W = 4864
Z_NG = PREP_W


def _params(sem, flags=None):
    return pltpu.CompilerParams(dimension_semantics=sem, vmem_limit_bytes=VMEM_LIMIT, flags=flags)


def _dot(a, b):
    return jnp.dot(a, b, preferred_element_type=F32)


def _dot_nt(a, b):
    return lax.dot_general(a, b, (((1,), (1,)), ((), ())), preferred_element_type=F32)


def _rms(x, eps=NORM_EPS):
    return x * lax.rsqrt(jnp.mean(x * x, axis=-1, keepdims=True) + eps)


def _split_dot(a, m_bf16, terms):
    acc = None
    rem = a
    for t in range(terms):
        piece = rem.astype(BF16)
        d = _dot(piece, m_bf16)
        acc = d if acc is None else acc + d
        if t + 1 < terms:
            rem = rem - piece.astype(F32)
    return acc


def _in_proj_kernel(x_ref, g_ref, w_ref, o_ref, h_ref):
    @pl.when(pl.program_id(1) == 0)
    def _():
        h_ref[...] = (_rms(x_ref[...]) * g_ref[...]).astype(BF16)

    o_ref[...] = _dot(h_ref[...], w_ref[...])


def _in_proj(x2d, g, w, tm, tn):
    n, d = x2d.shape
    nout = w.shape[1]
    return pl.pallas_call(
        _in_proj_kernel,
        out_shape=jax.ShapeDtypeStruct((n, nout), F32),
        grid=(n // tm, nout // tn),
        in_specs=[
            pl.BlockSpec((tm, d), lambda i, j: (i, 0)),
            pl.BlockSpec((1, d), lambda i, j: (0, 0)),
            pl.BlockSpec((d, tn), lambda i, j: (0, j)),
        ],
        out_specs=pl.BlockSpec((tm, tn), lambda i, j: (i, j)),
        scratch_shapes=[pltpu.VMEM((tm, d), BF16)],
        compiler_params=_params(("parallel", "arbitrary")),
        name="in_proj",
    )(x2d, g, w)


def _prep_kernel(z_ref, ra_ref, rb_ref, rc_ref, gain_ref, seg_ref,
                 qdt_ref, kd_ref, vdt_ref, nqpt_ref, nqrt_ref, ks_ref, vst_ref, kw_ref, vwt_ref):
    ra, rb, rc = ra_ref[...], rb_ref[...], rc_ref[...]
    seg = seg_ref[...]
    tm = ra.shape[0]
    nt = tm // LANES

    def rope(y):
        return (y * ra + pltpu.roll(y, LANES - ROT_HALF, 1) * rb + pltpu.roll(y, ROT_HALF, 1) * rc)

    def head_norm(xb, gain):
        ss = _split_dot(xb * xb, seg, 2)
        return xb * lax.rsqrt(ss * (1.0 / NSA_HEAD_DIM) + NORM_EPS) * gain

    def zblk(col):
        return z_ref[:, col:col + LANES]

    for hb in range(DIFF_W // LANES):
        c = hb * LANES
        qd = rope(head_norm(zblk(Z_DQ + c), gain_ref[0:1, :])) * QK_SCALE
        qdt_ref[c:c + LANES, :] = qd.T.astype(BF16)
        kd_ref[:, c:c + LANES] = rope(head_norm(zblk(Z_DK + c), gain_ref[1:2, :])).astype(BF16)
        vdt_ref[c:c + LANES, :] = zblk(Z_DV + c).T.astype(BF16)
        yq = head_norm(zblk(Z_NQ + c), gain_ref[2:3, :])
        qpt = (yq * QK_SCALE).T.astype(BF16)
        qrt = (rope(yq) * QK_SCALE).T.astype(BF16)
        for j in range(nt):
            nqpt_ref[j, c:c + LANES, :] = qpt[:, j * LANES:(j + 1) * LANES]
            nqrt_ref[j, c:c + LANES, :] = qrt[:, j * LANES:(j + 1) * LANES]

    ks_ref[...] = rope(head_norm(zblk(Z_KS), gain_ref[3:4, :])).astype(BF16)
    kw_ref[...] = rope(head_norm(zblk(Z_KW), gain_ref[3:4, :])).astype(BF16)
    for col, v_out in ((Z_VS, vst_ref), (Z_VW, vwt_ref)):
        vt = zblk(col).T.astype(BF16)
        for g in range(NSA_KV_GROUPS):
            for j in range(nt):
                v_out[g, j] = vt[g * NSA_HEAD_DIM:(g + 1) * NSA_HEAD_DIM, j * LANES:(j + 1) * LANES]


def _prep(z, ra, rb, rc, gains, seg, tm, t_diff):
    n = z.shape[0]
    nt = tm // LANES
    per = t_diff // tm
    row = lambda i: (i, 0)
    tile_t = lambda i: (i // per, 0, i % per)
    diff_t = jax.ShapeDtypeStruct((n // t_diff, DIFF_W, t_diff), BF16)
    nsa_qt = jax.ShapeDtypeStruct((n // LANES, NSA_W, LANES), BF16)
    nsa_vt = jax.ShapeDtypeStruct((NSA_KV_GROUPS, n // LANES, NSA_HEAD_DIM, LANES), BF16)
    diff_t_spec = pl.BlockSpec((None, DIFF_W, tm), tile_t)
    nsa_qt_spec = pl.BlockSpec((nt, NSA_W, LANES), lambda i: (i, 0, 0))
    nsa_vt_spec = pl.BlockSpec((NSA_KV_GROUPS, nt, NSA_HEAD_DIM, LANES), lambda i: (0, i, 0, 0))
    return pl.pallas_call(
        _prep_kernel,
        out_shape=[diff_t, jax.ShapeDtypeStruct((n, DIFF_W), BF16), diff_t, nsa_qt, nsa_qt,
                   jax.ShapeDtypeStruct((n, LANES), BF16), nsa_vt,
                   jax.ShapeDtypeStruct((n, LANES), BF16), nsa_vt],
        grid=(n // tm,),
        in_specs=[
            pl.BlockSpec((tm, PREP_W), row),
            pl.BlockSpec((tm, LANES), row),
            pl.BlockSpec((tm, LANES), row),
            pl.BlockSpec((tm, LANES), row),
            pl.BlockSpec((8, LANES), lambda i: (0, 0)),
            pl.BlockSpec((LANES, LANES), lambda i: (0, 0)),
        ],
        out_specs=[diff_t_spec, pl.BlockSpec((tm, DIFF_W), row), diff_t_spec, nsa_qt_spec, nsa_qt_spec,
                   pl.BlockSpec((tm, LANES), row), nsa_vt_spec,
                   pl.BlockSpec((tm, LANES), row), nsa_vt_spec],
        compiler_params=_params(("parallel",)),
        name="prep",
    )(z, ra, rb, rc, gains, seg)


def _compress_kernel(x_ref, pos_ref, w1_ref, w2_ref, gain_ref, o_ref, ot_ref):
    x = x_ref[...]
    nc = x.shape[0]
    a = (x + pos_ref[0:1, :]).astype(BF16)
    b = (x + pos_ref[1:2, :]).astype(BF16)
    u = _dot(a, w1_ref[0])
    v = _dot(b, w1_ref[1])
    pre = u + pltpu.roll(v, nc - 1, 0)
    cdf = 0.5 * (1.0 + jnp.tanh(math.sqrt(2.0 / math.pi) * (pre + 0.044715 * (pre * pre * pre))))
    c = _dot((pre * cdf).astype(BF16), w2_ref[...])
    is_key = pl.program_id(0) == 0
    out = jnp.where(is_key, _rms(c) * gain_ref[...], c)
    o_ref[...] = out.astype(BF16)
    ot_ref[...] = out.T.astype(BF16)


def _compress(xc, pos, w1, w2d, gain):
    _, b, g, nc, cw = xc.shape
    return pl.pallas_call(
        _compress_kernel,
        out_shape=[jax.ShapeDtypeStruct((2, b, g, nc, LANES), BF16),
                   jax.ShapeDtypeStruct((2, b, g, LANES, nc), BF16)],
        grid=(2, b, g),
        in_specs=[
            pl.BlockSpec((None, None, None, nc, cw), lambda t, i, j: (t, i, j, 0, 0)),
            pl.BlockSpec((None, 2, cw), lambda t, i, j: (t, 0, 0)),
            pl.BlockSpec((None, 2, cw, CMP_HIDDEN), lambda t, i, j: (t, 0, 0, 0)),
            pl.BlockSpec((None, CMP_HIDDEN, LANES), lambda t, i, j: (t, 0, 0)),
            pl.BlockSpec((1, LANES), lambda t, i, j: (0, 0)),
        ],
        out_specs=[pl.BlockSpec((None, None, None, nc, LANES), lambda t, i, j: (t, i, j, 0, 0)),
                   pl.BlockSpec((None, None, None, LANES, nc), lambda t, i, j: (t, i, j, 0, 0))],
        compiler_params=_params(("parallel", "parallel", "parallel")),
        name="compress",
    )(xc, pos, w1, w2d, gain)


def _online_update(s, vt, m_ref, l_ref, acc_ref, cols=None):
    cols = slice(None) if cols is None else cols
    m_old = m_ref[:, cols]
    m_new = jnp.maximum(m_old, jnp.max(s, axis=0, keepdims=True))
    alpha = jnp.exp(m_old - m_new)
    p = jnp.exp(s - m_new)
    l_ref[:, cols] = alpha * l_ref[:, cols] + jnp.sum(p, axis=0, keepdims=True)
    acc_ref[:, cols] = alpha * acc_ref[:, cols] + _dot(vt, p.astype(BF16))
    m_ref[:, cols] = m_new


def _diff_attn_kernel(lam_ref, qt_ref, k_ref, vt_ref, g_ref, o_ref,
                      m1, l1, a1, m2, l2, a2, sa_ref, sb_ref, *, lambda_init, t):
    qi = pl.program_id(2)
    qt = qt_ref[...]
    row = lax.broadcasted_iota(jnp.int32, qt.shape, 0)
    zero = jnp.zeros_like(qt)
    q_sub = (jnp.where(row < DIFF_SUB_DIM, qt, zero), jnp.where(row >= DIFF_SUB_DIM, qt, zero))
    stats = ((m1, l1, a1), (m2, l2, a2))
    for m_ref, l_ref, a_ref in stats:
        m_ref[...] = jnp.full(m_ref.shape, NEG_INF, F32)
        l_ref[...] = jnp.zeros(l_ref.shape, F32)
        a_ref[...] = jnp.zeros(a_ref.shape, F32)

    def scores(kt, s_ref):
        k = k_ref[pl.ds(pl.multiple_of(kt * t, t), t), :]
        for i in range(2):
            s_ref[i] = _dot(k, q_sub[i])

    def absorb(kt, s_ref, causal):
        vt = vt_ref[kt]
        for i in range(2):
            s = s_ref[i]
            if causal:
                kpos = lax.broadcasted_iota(jnp.int32, s.shape, 0)
                qpos = lax.broadcasted_iota(jnp.int32, s.shape, 1)
                s = jnp.where(kpos <= qpos, s, NEG_INF)
            _online_update(s, vt, *stats[i])

    scores(0, sa_ref)

    def body(j, carry):
        scores(2 * j + 1, sb_ref)
        absorb(2 * j, sa_ref, False)
        scores(2 * j + 2, sa_ref)
        absorb(2 * j + 1, sb_ref, False)
        return carry

    lax.fori_loop(0, qi // 2, body, 0)

    @pl.when(qi % 2 == 0)
    def _():
        absorb(qi, sa_ref, True)

    @pl.when(qi % 2 == 1)
    def _():
        scores(qi, sb_ref)
        absorb(qi - 1, sa_ref, False)
        absorb(qi, sb_ref, True)

    lp = lam_ref[...]
    lam = (jnp.exp(jnp.sum(lp[0:1] * lp[1:2], axis=-1, keepdims=True))
           - jnp.exp(jnp.sum(lp[2:3] * lp[3:4], axis=-1, keepdims=True)) + lambda_init)
    o = (a1[...] / l1[...] - lam * (a2[...] / l2[...])).T
    o_ref[...] = (_rms(o) * g_ref[...] * (1.0 - lambda_init)).astype(BF16)


def _diff_attn(lam_p, qdt, kd, vdt, subln, lambda_init, t):
    b, s, _ = kd.shape
    nq = s // t
    kernel = functools.partial(_diff_attn_kernel, lambda_init=lambda_init, t=t)
    stat = pltpu.VMEM((1, t), F32)
    acc = pltpu.VMEM((DIFF_V_DIM, t), F32)
    score_buf = pltpu.VMEM((2, t, t), F32)
    return pl.pallas_call(
        kernel,
        out_shape=jax.ShapeDtypeStruct((b, s, DIFF_W), BF16),
        grid=(b, DIFF_HEADS, nq),
        in_specs=[
            pl.BlockSpec((4, DIFF_SUB_DIM), lambda i, h, j: (0, 0)),
            pl.BlockSpec((None, LANES, t), lambda i, h, j: (i * nq + j, h, 0)),
            pl.BlockSpec((None, s, LANES), lambda i, h, j: (i, 0, h)),
            pl.BlockSpec((nq, LANES, t), lambda i, h, j: (i, h, 0)),
            pl.BlockSpec((1, LANES), lambda i, h, j: (0, 0)),
        ],
        out_specs=pl.BlockSpec((None, t, LANES), lambda i, h, j: (i, j, h)),
        scratch_shapes=[stat, stat, acc, stat, stat, acc, score_buf, score_buf],
        compiler_params=_params(("parallel", "parallel", "arbitrary")),
        name="diff_attn",
    )(lam_p, qdt, kd, vdt, subln)


def _nsa_kernel(qpt_ref, qrt_ref, kc_ref, vct_ref, ks_ref, vst_ref, kw_ref, vwt_ref, ng_ref,
                ovlt_ref, o_ref, m_ref, l_ref, acc_ref, part_ref, bias_ref, sa_ref, sb_ref,
                *, tq, tk, n_sel, top_n, wlen):
    r_heads = NSA_GROUP_SIZE
    hd = NSA_HEAD_DIM
    g = pl.program_id(1)
    q0 = pl.program_id(2) * tq
    qpos = q0 + lax.broadcasted_iota(jnp.int32, (1, tq), 1)

    def heads(x):
        return jnp.concatenate([x] * r_heads, axis=1)

    def head_cols(x, r):
        return x[:, r * tq:(r + 1) * tq]

    def stack_q(qt_ref):
        qt = qt_ref[...]
        q64 = jnp.concatenate([qt[r * hd:(r + 1) * hd, :] for r in range(r_heads)], axis=1)
        q128 = jnp.concatenate([q64, q64], axis=0)
        half = lax.broadcasted_iota(jnp.int32, q128.shape, 0) // hd
        return jnp.where(half == g, q128, jnp.zeros_like(q128))

    q_plain = stack_q(qpt_ref)
    q_rot = stack_q(qrt_ref)

    ncp = kc_ref.shape[0]
    n_idx = lax.broadcasted_iota(jnp.int32, (ncp, tq), 0)
    c_ok = (n_idx * CMP_STRIDE + (CMP_BLOCK - 1)) <= qpos
    s_c = _dot(kc_ref[...], q_plain) + heads(jnp.where(c_ok, 0.0, NEG_INF))
    e_c = jnp.exp(s_c - jnp.max(s_c, axis=0, keepdims=True))
    p_c = e_c / jnp.sum(e_c, axis=0, keepdims=True) * heads(jnp.where(c_ok, 1.0, 0.0))
    o_c = _dot(vct_ref[0:hd, :], p_c.astype(BF16))

    p_sum = head_cols(p_c, 0)
    for r in range(1, r_heads):
        p_sum = p_sum + head_cols(p_c, r)
    ovlt = ovlt_ref[...]
    nb = ovlt.shape[0]
    imp, rem = None, p_sum
    for term in range(3):
        piece = rem.astype(BF16)
        d = _dot(ovlt, piece)
        imp = d if imp is None else imp + d
        rem = rem - piece.astype(F32)
    blk = lax.broadcasted_iota(jnp.int32, (nb, tq), 0)
    q_blk = qpos // SLC_BLOCK
    valid = blk <= q_blk
    forced = (blk == 0) | (blk == q_blk) | (blk == q_blk - 1)
    score = jnp.where(valid, imp + SLC_FORCED_BONUS * jnp.where(forced, 1.0, 0.0), -1.0)
    rank = jnp.zeros((nb, tq), F32)
    for jp in range(n_sel):
        other = score[jp:jp + 1, :]
        gt = jnp.where(other > score, 1.0, 0.0)
        ge = jnp.where(other >= score, 1.0, 0.0)
        rank = rank + jnp.where(blk > jp, ge, gt)
    bias_ref[...] = jnp.where(valid & (rank < float(top_n)), 0.0, NEG_INF)

    start = pl.multiple_of(jnp.maximum(q0 + tq - wlen, 0), LANES)
    kw = kw_ref[pl.ds(start, wlen), :]
    vwt = jnp.concatenate([vwt_ref[start // LANES + j] for j in range(wlen // LANES)], axis=1)
    dist = qpos - (start + lax.broadcasted_iota(jnp.int32, (wlen, tq), 0))
    w_ok = (dist >= 0) & (dist < WINDOW)
    s_w = _dot(kw, q_rot) + heads(jnp.where(w_ok, 0.0, NEG_INF))
    e_w = jnp.exp(s_w - jnp.max(s_w, axis=0, keepdims=True))
    o_w = _dot(vwt, e_w.astype(BF16)) / jnp.sum(e_w, axis=0, keepdims=True)

    gt_all = jax.nn.sigmoid(ng_ref[...]).T

    def gate(r, c):
        return gt_all[3 * r + c:3 * r + c + 1, :]

    for r in range(r_heads):
        part_ref[r * hd:(r + 1) * hd, :] = gate(r, 0) * head_cols(o_c, r) + gate(r, 2) * head_cols(o_w, r)

    m_ref[...] = jnp.full(m_ref.shape, NEG_INF, F32)
    l_ref[...] = jnp.zeros(l_ref.shape, F32)
    acc_ref[...] = jnp.zeros(acc_ref.shape, F32)
    sub = tk // LANES
    blk_per_tile = tk // SLC_BLOCK

    def scores(kt, s_ref):
        s_ref[...] = _dot(ks_ref[pl.ds(pl.multiple_of(kt * tk, tk), tk), :], q_rot)

    def absorb(kt, s_ref, causal):
        vt = jnp.concatenate([vst_ref[kt * sub + j] for j in range(sub)], axis=1)
        bias = jnp.concatenate(
            [jnp.broadcast_to(bias_ref[pl.ds(kt * blk_per_tile + i, 1), :], (SLC_BLOCK, tq))
             for i in range(blk_per_tile)], axis=0)
        if causal:
            kpos = kt * tk + lax.broadcasted_iota(jnp.int32, (tk, tq), 0)
            bias = jnp.where(kpos <= qpos, bias, NEG_INF)
        _online_update(s_ref[...] + heads(bias), vt, m_ref, l_ref, acc_ref)

    kt_last = q0 // tk
    scores(0, sa_ref)

    def sel_body(j, carry):
        scores(2 * j + 1, sb_ref)
        absorb(2 * j, sa_ref, False)
        scores(2 * j + 2, sa_ref)
        absorb(2 * j + 1, sb_ref, False)
        return carry

    lax.fori_loop(0, kt_last // 2, sel_body, 0)

    @pl.when(kt_last % 2 == 0)
    def _():
        absorb(kt_last, sa_ref, True)

    @pl.when(kt_last % 2 == 1)
    def _():
        scores(kt_last, sb_ref)
        absorb(kt_last - 1, sa_ref, False)
        absorb(kt_last, sb_ref, True)

    o_s = acc_ref[...] / l_ref[...]
    outs = [part_ref[r * hd:(r + 1) * hd, :] + gate(r, 1) * head_cols(o_s, r) for r in range(r_heads)]
    o_ref[...] = jnp.concatenate(outs, axis=0).T.astype(BF16)


def _nsa_attn(nqpt, nqrt, cmp_kv, cmp_kvt, ks, vst, kw, vwt, z, ovlt, b, s, tk):
    tq = LANES
    n_sel = s // SLC_BLOCK
    top_n = min(SLC_TOP_N, n_sel)
    wlen = min(WINDOW + tq, s)
    ncp = cmp_kv.shape[3]
    gw = NSA_GROUP_SIZE * NSA_HEAD_DIM
    nq = s // tq
    kernel = functools.partial(_nsa_kernel, tq=tq, tk=tk, n_sel=n_sel, top_n=top_n, wlen=wlen)
    qt_spec = pl.BlockSpec((None, gw, tq), lambda i, g, j: (i * nq + j, g, 0))
    k_spec = pl.BlockSpec((s, LANES), lambda i, g, j: (i, 0))
    vt_spec = pl.BlockSpec((None, s // LANES, NSA_HEAD_DIM, LANES), lambda i, g, j: (g, i, 0, 0))
    cols = NSA_GROUP_SIZE * tq
    return pl.pallas_call(
        kernel,
        out_shape=jax.ShapeDtypeStruct((b, s, NSA_W), BF16),
        grid=(b, NSA_KV_GROUPS, nq),
        in_specs=[
            qt_spec, qt_spec,
            pl.BlockSpec((None, None, None, ncp, LANES), lambda i, g, j: (0, i, g, 0, 0)),
            pl.BlockSpec((None, None, None, LANES, ncp), lambda i, g, j: (1, i, g, 0, 0)),
            k_spec, vt_spec, k_spec, vt_spec,
            pl.BlockSpec((tq, LANES), lambda i, g, j: (i * nq + j, Z_NG // LANES + g)),
            pl.BlockSpec(ovlt.shape, lambda i, g, j: (0, 0)),
        ],
        out_specs=pl.BlockSpec((None, tq, gw), lambda i, g, j: (i, j, g)),
        scratch_shapes=[pltpu.VMEM((1, cols), F32), pltpu.VMEM((1, cols), F32),
                        pltpu.VMEM((NSA_HEAD_DIM, cols), F32), pltpu.VMEM((gw, tq), F32),
                        pltpu.VMEM((n_sel, tq), F32),
                        pltpu.VMEM((tk, cols), F32), pltpu.VMEM((tk, cols), F32)],
        compiler_params=_params(("parallel", "parallel", "arbitrary")),
        name="nsa_attn",
    )(nqpt, nqrt, cmp_kv, cmp_kvt, ks, vst, kw, vwt, z, ovlt)


def _merge_out_kernel(x_ref, ya_ref, yb_ref, ga_ref, gb_ref, wa_ref, wb_ref, wo_ref, o_ref, mg_ref,
                      *, tn):
    d = x_ref.shape[1]
    ya = ya_ref[...]
    yb = yb_ref[...]
    for c in range(0, d, tn):
        ta = _dot(ya, wa_ref[:, c:c + tn])
        tb = _dot(yb, wb_ref[:, c:c + tn])
        mg_ref[:, c:c + tn] = (jax.nn.sigmoid(ga_ref[:, c:c + tn]) * ta
                               + jax.nn.sigmoid(gb_ref[:, c:c + tn]) * tb).astype(BF16)
    mg = mg_ref[...]
    for c in range(0, d, tn):
        o_ref[:, c:c + tn] = x_ref[:, c:c + tn] + _dot(mg, wo_ref[:, c:c + tn])


def _merge_out(x2d, ya, yb, zg, wa, wb, wo, tm, tn):
    n, d = x2d.shape
    row = lambda i: (i, 0)
    const = lambda i: (0, 0)
    resident = pl.Buffered(1)
    return pl.pallas_call(
        functools.partial(_merge_out_kernel, tn=tn),
        out_shape=jax.ShapeDtypeStruct((n, d), F32),
        grid=(n // tm,),
        in_specs=[
            pl.BlockSpec((tm, d), row),
            pl.BlockSpec((tm, DIFF_W), row),
            pl.BlockSpec((tm, NSA_W), row),
            pl.BlockSpec((tm, d), lambda i: (i, 0)),
            pl.BlockSpec((tm, d), lambda i: (i, 1)),
            pl.BlockSpec((DIFF_W, d), const, pipeline_mode=resident),
            pl.BlockSpec((NSA_W, d), const, pipeline_mode=resident),
            pl.BlockSpec((d, d), const, pipeline_mode=resident),
        ],
        out_specs=pl.BlockSpec((tm, d), row),
        scratch_shapes=[pltpu.VMEM((tm, d), BF16)],
        compiler_params=_params(("parallel",)),
        name="merge_out",
    )(x2d, ya, yb, zg, zg, wa, wb, wo)


def _mlp_kernel(x_ref, g_ref, wu_ref, wd_ref, o_ref, h_ref, acc_ref):
    f = pl.program_id(1)

    @pl.when(f == 0)
    def _():
        h_ref[...] = (_rms(x_ref[...]) * g_ref[...]).astype(BF16)
        acc_ref[...] = jnp.zeros(acc_ref.shape, F32)

    u = jnp.maximum(_dot(h_ref[...], wu_ref[...]), 0.0)
    acc_ref[...] += _dot((u * u).astype(BF16), wd_ref[...])

    @pl.when(f == pl.num_programs(1) - 1)
    def _():
        o_ref[...] = x_ref[...] + acc_ref[...]


def _mlp(x2d, g, wu, wd, tm, tf):
    n, d = x2d.shape
    ff = wu.shape[1]
    return pl.pallas_call(
        _mlp_kernel,
        out_shape=jax.ShapeDtypeStruct((n, d), F32),
        grid=(n // tm, ff // tf),
        in_specs=[
            pl.BlockSpec((tm, d), lambda i, f: (i, 0)),
            pl.BlockSpec((1, d), lambda i, f: (0, 0)),
            pl.BlockSpec((d, tf), lambda i, f: (0, f)),
            pl.BlockSpec((tf, d), lambda i, f: (f, 0)),
        ],
        out_specs=pl.BlockSpec((tm, d), lambda i, f: (i, 0)),
        scratch_shapes=[pltpu.VMEM((tm, d), BF16), pltpu.VMEM((tm, d), F32)],
        compiler_params=_params(("parallel", "arbitrary")),
        name="mlp",
    )(x2d, g, wu, wd)


def _ple_kernel(x_ref, p_ref, g_ref, wp_ref, wg_ref, o_ref):
    x = x_ref[...]
    e = _rms(_dot(p_ref[...].astype(BF16), wp_ref[...])) * g_ref[...]
    gate = jax.nn.sigmoid(_dot(_rms(x).astype(BF16), wg_ref[...]))
    o_ref[...] = x + gate * e


def _ple(x2d, p2d, g, wp, wg, tm):
    n, d = x2d.shape
    row = lambda i: (i, 0)
    const = lambda i: (0, 0)
    resident = pl.Buffered(1)
    return pl.pallas_call(
        _ple_kernel,
        out_shape=jax.ShapeDtypeStruct((n, d), F32),
        grid=(n // tm,),
        in_specs=[
            pl.BlockSpec((tm, d), row),
            pl.BlockSpec((tm, PLE_DIM), row),
            pl.BlockSpec((1, d), const),
            pl.BlockSpec((PLE_DIM, d), const, pipeline_mode=resident),
            pl.BlockSpec((d, d), const, pipeline_mode=resident),
        ],
        out_specs=pl.BlockSpec((tm, d), row),
        compiler_params=_params(("parallel",)),
        name="ple",
    )(x2d, p2d, g, wp, wg)


def _rope_tables(positions):
    rot = 2 * ROT_HALF
    inv_freq = jnp.power(ROPE_THETA, -jnp.arange(0, rot, 2, dtype=F32) / rot)
    ang = positions.astype(F32)[..., None] * inv_freq
    cos, sin = jnp.cos(ang), jnp.sin(ang)
    n = cos.shape[0] * cos.shape[1]
    cos, sin = cos.reshape(n, ROT_HALF), sin.reshape(n, ROT_HALF)
    rest = NSA_HEAD_DIM - rot
    ones, zeros_r, zeros_h = jnp.ones((n, rest), F32), jnp.zeros((n, rest), F32), jnp.zeros((n, ROT_HALF), F32)
    ra = jnp.concatenate([cos, cos, ones], axis=1)
    rb = jnp.concatenate([-sin, zeros_h, zeros_r], axis=1)
    rc = jnp.concatenate([zeros_h, sin, zeros_r], axis=1)
    return tuple(jnp.tile(t, (1, LANES // NSA_HEAD_DIM)) for t in (ra, rb, rc))


def _regroup_w_in(w_in):
    d = w_in.shape[0]
    ng0 = 3 * DIFF_W + NSA_W + 6 * NSA_KV_W
    per_g = 3 * NSA_GROUP_SIZE
    parts = [w_in[:, :ng0]]
    for g in range(NSA_KV_GROUPS):
        parts += [w_in[:, ng0 + g * per_g:ng0 + (g + 1) * per_g], jnp.zeros((d, LANES - per_g), w_in.dtype)]
    w_main = jnp.concatenate(parts, axis=1).astype(BF16)
    w_gate = w_in[:, ng0 + 3 * NSA_HEADS:].astype(BF16)
    return w_main, w_gate


def _pick(n, pref):
    t = min(pref, n)
    while n % t:
        t //= 2
    return t


def kernel(x, p, positions, norm_mix, w_in, diff_q_norm, diff_k_norm, diff_lambda, diff_subln,
           nsa_q_norm, nsa_k_norm, cmp_pos, cmp_w1, cmp_w2, w_proj_diff, w_proj_nsa, w_out,
           norm_mlp, w_mlp_up, w_mlp_down, w_ple_proj, norm_ple, w_ple_gate):
    b, s, d = x.shape
    n = b * s
    assert d == D_MODEL and s % 512 == 0
    ra, rb, rc = _rope_tables(positions)
    seg = (jnp.arange(LANES)[:, None] // NSA_HEAD_DIM == jnp.arange(LANES)[None, :] // NSA_HEAD_DIM).astype(BF16)
    ncp = s // CMP_STRIDE
    n_sel = s // SLC_BLOCK
    assert n_sel % 16 == 0 and n_sel <= LANES
    cs = jnp.arange(ncp)[None, :] * CMP_STRIDE
    ss = jnp.arange(n_sel)[:, None] * SLC_BLOCK
    ovlt = ((cs < ss + SLC_BLOCK) & (cs + CMP_BLOCK > ss)).astype(BF16)
    tk_nsa, t_diff = 512, 512

    x2d = x.reshape(n, d)
    for i in range(p.shape[0]):
        lambda_init = 0.8 - 0.6 * math.exp(-0.3 * i)
        dup = lambda v: jnp.tile(v, LANES // NSA_HEAD_DIM)
        gains = jnp.zeros((8, LANES), F32).at[0].set(dup(diff_q_norm[i])).at[1].set(dup(diff_k_norm[i]))
        gains = gains.at[2].set(dup(nsa_q_norm[i])).at[3].set(dup(nsa_k_norm[i]))

        w_main, w_gate = _regroup_w_in(w_in[i])
        z = _in_proj(x2d, norm_mix[i][None], w_main, _pick(n, 512), 512)
        zg = _in_proj(x2d, norm_mix[i][None], w_gate, _pick(n, 512), 512)
        qdt, kd, vdt, nqpt, nqrt, ks, vst, kw, vwt = _prep(z, ra, rb, rc, gains, seg, _pick(t_diff, 256), t_diff)

        def chunks(col):
            a = z[:, col:col + NSA_KV_W].reshape(b, ncp, CMP_STRIDE, NSA_KV_GROUPS, NSA_HEAD_DIM)
            return a.transpose(0, 3, 1, 2, 4).reshape(b, NSA_KV_GROUPS, ncp, CMP_STRIDE * NSA_HEAD_DIM)

        half = CMP_STRIDE * NSA_HEAD_DIM
        cmp_kv, cmp_kvt = _compress(
            jnp.stack([chunks(Z_KC), chunks(Z_VC)]),
            cmp_pos[i].reshape(2, 2, half),
            cmp_w1[i].reshape(2, 2, half, CMP_HIDDEN).astype(BF16),
            jnp.tile(cmp_w2[i], (1, 1, LANES // NSA_HEAD_DIM)).astype(BF16),
            gains[3:4])

        ya = _diff_attn(diff_lambda[i], qdt, kd.reshape(b, s, DIFF_W), vdt, diff_subln[i][None],
                        lambda_init, t_diff)
        yb = _nsa_attn(nqpt, nqrt, cmp_kv, cmp_kvt, ks, vst, kw, vwt, z, ovlt, b, s, tk_nsa)

        x2d = _merge_out(x2d, ya.reshape(n, DIFF_W), yb.reshape(n, NSA_W), zg,
                         w_proj_diff[i].astype(BF16), w_proj_nsa[i].astype(BF16), w_out[i].astype(BF16),
                         _pick(n, 256), 512)
        x2d = _mlp(x2d, norm_mlp[i][None], w_mlp_up[i].astype(BF16), w_mlp_down[i].astype(BF16),
                   _pick(n, 512), 512)
        x2d = _ple(x2d, p[i].reshape(n, PLE_DIM), norm_ple[i][None], w_ple_proj[i].astype(BF16),
                   w_ple_gate[i].astype(BF16), _pick(n, 256))
    return x2d.reshape(b, s, d)
```

```python
import functools
import math

import jax
import jax.numpy as jnp
from jax import lax
from jax.experimental import pallas as pl
from jax.experimental.pallas import tpu as pltpu

F32 = jnp.float32
BF16 = jnp.bfloat16

D_MODEL = 2048
PLE_DIM = 256
ROPE_THETA = 500000.0
ROPE_FRACTION = 4
NORM_EPS = 1e-6
NEG_INF = -1e30
DIFF_HEADS = 8
DIFF_SUB_DIM = 64
DIFF_V_DIM = 2 * DIFF_SUB_DIM
NSA_HEADS = 16
NSA_KV_GROUPS = 2
NSA_GROUP_SIZE = NSA_HEADS // NSA_KV_GROUPS
NSA_HEAD_DIM = 64
CMP_BLOCK = 32
CMP_STRIDE = 16
CMP_HIDDEN = 256
SLC_BLOCK = 64
SLC_TOP_N = 16
SLC_FORCED_BONUS = 1e4
WINDOW = 512
D_FF = 4 * D_MODEL
DIFF_W = DIFF_HEADS * DIFF_V_DIM
NSA_W = NSA_HEADS * NSA_HEAD_DIM
NSA_KV_W = NSA_KV_GROUPS * NSA_HEAD_DIM
ROT_HALF = NSA_HEAD_DIM // ROPE_FRACTION // 2
QK_SCALE = NSA_HEAD_DIM ** -0.5
Q_SCALE = QK_SCALE * math.log2(math.e)
SUM_ROWS = 16

LANES = 128
VMEM_LIMIT = 56 * 1024 * 1024

Z_DQ, Z_DK, Z_DV, Z_NQ = 0, 1024, 2048, 3072
Z_KC, Z_VC, Z_KS, Z_VS, Z_KW, Z_VW = 4096, 4224, 4352, 4480, 4608, 4736
PREP_W = 4864
Z_NG = PREP_W


def _params(sem, flags=None):
    return pltpu.CompilerParams(dimension_semantics=sem, vmem_limit_bytes=VMEM_LIMIT, flags=flags)


def _dot(a, b):
    return jnp.dot(a, b, preferred_element_type=F32)


def _dot_nt(a, b):
    return lax.dot_general(a, b, (((1,), (1,)), ((), ())), preferred_element_type=F32)


def _rms(x, eps=NORM_EPS):
    return x * lax.rsqrt(jnp.mean(x * x, axis=-1, keepdims=True) + eps)


def _split_dot(a, m_bf16, terms):
    acc = None
    rem = a
    for t in range(terms):
        piece = rem.astype(BF16)
        d = _dot(piece, m_bf16)
        acc = d if acc is None else acc + d
        if t + 1 < terms:
            rem = rem - piece.astype(F32)
    return acc


def _in_proj_kernel(x_ref, g_ref, w_ref, o_ref, h_ref):
    @pl.when(pl.program_id(1) == 0)
    def _():
        h_ref[...] = (_rms(x_ref[...]) * g_ref[...]).astype(BF16)

    o_ref[...] = _dot(h_ref[...], w_ref[...])


def _in_proj(x2d, g, w, tm, tn):
    n, d = x2d.shape
    nout = w.shape[1]
    return pl.pallas_call(
        _in_proj_kernel,
        out_shape=jax.ShapeDtypeStruct((n, nout), F32),
        grid=(n // tm, nout // tn),
        in_specs=[
            pl.BlockSpec((tm, d), lambda i, j: (i, 0)),
            pl.BlockSpec((1, d), lambda i, j: (0, 0)),
            pl.BlockSpec((d, tn), lambda i, j: (0, j)),
        ],
        out_specs=pl.BlockSpec((tm, tn), lambda i, j: (i, j)),
        scratch_shapes=[pltpu.VMEM((tm, d), BF16)],
        compiler_params=_params(("parallel", "arbitrary")),
        name="in_proj",
    )(x2d, g, w)


def _prep_kernel(z_ref, ra_ref, rb_ref, rc_ref, gain_ref, seg_ref,
                 qdt_ref, kd_ref, vdt_ref, nqpt_ref, nqrt_ref, ks_ref, vst_ref, kw_ref, vwt_ref):
    ra, rb, rc = ra_ref[...], rb_ref[...], rc_ref[...]
    seg = seg_ref[...]
    tm = ra.shape[0]
    nt = tm // LANES

    def rope(y):
        return (y * ra + pltpu.roll(y, LANES - ROT_HALF, 1) * rb + pltpu.roll(y, ROT_HALF, 1) * rc)

    def head_norm(xb, gain):
        ss = _split_dot(xb * xb, seg, 2)
        return xb * lax.rsqrt(ss * (1.0 / NSA_HEAD_DIM) + NORM_EPS) * gain

    def zblk(col):
        return z_ref[:, col:col + LANES]

    for hb in range(DIFF_W // LANES):
        c = hb * LANES
        qd = rope(head_norm(zblk(Z_DQ + c), gain_ref[0:1, :])) * Q_SCALE
        qdt_ref[c:c + LANES, :] = qd.T.astype(BF16)
        kd_ref[:, c:c + LANES] = rope(head_norm(zblk(Z_DK + c), gain_ref[1:2, :])).astype(BF16)
        vdt_ref[c:c + LANES, :] = zblk(Z_DV + c).T.astype(BF16)
        yq = head_norm(zblk(Z_NQ + c), gain_ref[2:3, :])
        qpt = (yq * Q_SCALE).T.astype(BF16)
        qrt = (rope(yq) * Q_SCALE).T.astype(BF16)
        for j in range(nt):
            nqpt_ref[j, c:c + LANES, :] = qpt[:, j * LANES:(j + 1) * LANES]
            nqrt_ref[j, c:c + LANES, :] = qrt[:, j * LANES:(j + 1) * LANES]

    ks_ref[...] = rope(head_norm(zblk(Z_KS), gain_ref[3:4, :])).astype(BF16)
    kw_ref[...] = rope(head_norm(zblk(Z_KW), gain_ref[3:4, :])).astype(BF16)
    hd = NSA_HEAD_DIM
    for col, v_out in ((Z_VS, vst_ref), (Z_VW, vwt_ref)):
        vt = zblk(col).T.astype(BF16)
        for g in range(NSA_KV_GROUPS):
            for j in range(nt):
                v_out[g, j] = vt[g * hd:(g + 1) * hd, j * LANES:(j + 1) * LANES]


def _prep(z, ra, rb, rc, gains, seg, tm, t_diff):
    n = z.shape[0]
    nt = tm // LANES
    per = t_diff // tm
    row = lambda i: (i, 0)
    tile_t = lambda i: (i // per, 0, i % per)
    diff_t = jax.ShapeDtypeStruct((n // t_diff, DIFF_W, t_diff), BF16)
    nsa_qt = jax.ShapeDtypeStruct((n // LANES, NSA_W, LANES), BF16)
    nsa_vt = jax.ShapeDtypeStruct((NSA_KV_GROUPS, n // LANES, NSA_HEAD_DIM, LANES), BF16)
    diff_t_spec = pl.BlockSpec((None, DIFF_W, tm), tile_t)
    nsa_qt_spec = pl.BlockSpec((nt, NSA_W, LANES), lambda i: (i, 0, 0))
    nsa_vt_spec = pl.BlockSpec((NSA_KV_GROUPS, nt, NSA_HEAD_DIM, LANES), lambda i: (0, i, 0, 0))
    return pl.pallas_call(
        _prep_kernel,
        out_shape=[diff_t, jax.ShapeDtypeStruct((n, DIFF_W), BF16), diff_t, nsa_qt, nsa_qt,
                   jax.ShapeDtypeStruct((n, LANES), BF16), nsa_vt,
                   jax.ShapeDtypeStruct((n, LANES), BF16), nsa_vt],
        grid=(n // tm,),
        in_specs=[
            pl.BlockSpec((tm, PREP_W), row),
            pl.BlockSpec((tm, LANES), row),
            pl.BlockSpec((tm, LANES), row),
            pl.BlockSpec((tm, LANES), row),
            pl.BlockSpec((8, LANES), lambda i: (0, 0)),
            pl.BlockSpec((LANES, LANES), lambda i: (0, 0)),
        ],
        out_specs=[diff_t_spec, pl.BlockSpec((tm, DIFF_W), row), diff_t_spec, nsa_qt_spec, nsa_qt_spec,
                   pl.BlockSpec((tm, LANES), row), nsa_vt_spec,
                   pl.BlockSpec((tm, LANES), row), nsa_vt_spec],
        compiler_params=_params(("parallel",)),
        name="prep",
    )(z, ra, rb, rc, gains, seg)


def _compress_kernel(x_ref, pos_ref, w1_ref, w2_ref, gain_ref, o_ref, ot_ref):
    x = x_ref[...]
    nc = x.shape[0]
    a = (x + pos_ref[0:1, :]).astype(BF16)
    b = (x + pos_ref[1:2, :]).astype(BF16)
    u = _dot(a, w1_ref[0])
    v = _dot(b, w1_ref[1])
    pre = u + pltpu.roll(v, nc - 1, 0)
    cdf = 0.5 * (1.0 + jnp.tanh(math.sqrt(2.0 / math.pi) * (pre + 0.044715 * (pre * pre * pre))))
    c = _dot((pre * cdf).astype(BF16), w2_ref[...])
    is_key = pl.program_id(0) == 0
    out = jnp.where(is_key, _rms(c) * gain_ref[...], c)
    o_ref[...] = out.astype(BF16)
    ot_ref[...] = out.T.astype(BF16)


def _compress(xc, pos, w1, w2d, gain):
    _, b, g, nc, cw = xc.shape
    return pl.pallas_call(
        _compress_kernel,
        out_shape=[jax.ShapeDtypeStruct((2, b, g, nc, LANES), BF16),
                   jax.ShapeDtypeStruct((2, b, g, LANES, nc), BF16)],
        grid=(2, b, g),
        in_specs=[
            pl.BlockSpec((None, None, None, nc, cw), lambda t, i, j: (t, i, j, 0, 0)),
            pl.BlockSpec((None, 2, cw), lambda t, i, j: (t, 0, 0)),
            pl.BlockSpec((None, 2, cw, CMP_HIDDEN), lambda t, i, j: (t, 0, 0, 0)),
            pl.BlockSpec((None, CMP_HIDDEN, LANES), lambda t, i, j: (t, 0, 0)),
            pl.BlockSpec((1, LANES), lambda t, i, j: (0, 0)),
        ],
        out_specs=[pl.BlockSpec((None, None, None, nc, LANES), lambda t, i, j: (t, i, j, 0, 0)),
                   pl.BlockSpec((None, None, None, LANES, nc), lambda t, i, j: (t, i, j, 0, 0))],
        compiler_params=_params(("parallel", "parallel", "parallel")),
        name="compress",
    )(xc, pos, w1, w2d, gain)


def _online_update(s, vt, m_ref, acc_ref):
    m_old = m_ref[...]
    m_new = jnp.maximum(m_old, jnp.max(s, axis=0, keepdims=True))
    p = jnp.exp2(s - m_new)
    acc_ref[...] = jnp.exp2(m_old - m_new) * acc_ref[...] + _dot(vt, p.astype(BF16))
    m_ref[...] = m_new


def _with_sum_rows(vt):
    rows = lax.broadcasted_iota(jnp.int32, (SUM_ROWS, vt.shape[1]), 0)
    return jnp.concatenate([vt, jnp.where(rows == 0, 1.0, 0.0).astype(vt.dtype)], axis=0)


def _normalized(acc, dv):
    return acc[0:dv, :] / acc[dv:dv + 1, :]


def _diff_attn_kernel(lam_ref, qt_ref, k_ref, vt_ref, g_ref, o_ref,
                      m1, a1, m2, a2, sa_ref, sb_ref, *, lambda_init, t):
    qi = pl.program_id(2)
    qt = qt_ref[...]
    row = lax.broadcasted_iota(jnp.int32, qt.shape, 0)
    zero = jnp.zeros_like(qt)
    q_sub = (jnp.where(row < DIFF_SUB_DIM, qt, zero), jnp.where(row >= DIFF_SUB_DIM, qt, zero))
    stats = ((m1, a1), (m2, a2))
    for m_ref, a_ref in stats:
        m_ref[...] = jnp.full(m_ref.shape, NEG_INF, F32)
        a_ref[...] = jnp.zeros(a_ref.shape, F32)

    def scores(kt, s_ref):
        k = k_ref[pl.ds(pl.multiple_of(kt * t, t), t), :]
        for i in range(2):
            s_ref[i] = _dot(k, q_sub[i])

    def absorb(kt, s_ref, causal):
        vt = _with_sum_rows(vt_ref[kt])
        for i in range(2):
            s = s_ref[i]
            if causal:
                kpos = lax.broadcasted_iota(jnp.int32, s.shape, 0)
                qpos = lax.broadcasted_iota(jnp.int32, s.shape, 1)
                s = jnp.where(kpos <= qpos, s, NEG_INF)
            _online_update(s, vt, *stats[i])

    scores(0, sa_ref)

    def body(j, carry):
        scores(2 * j + 1, sb_ref)
        absorb(2 * j, sa_ref, False)
        scores(2 * j + 2, sa_ref)
        absorb(2 * j + 1, sb_ref, False)
        return carry

    lax.fori_loop(0, qi // 2, body, 0)

    @pl.when(qi % 2 == 0)
    def _():
        absorb(qi, sa_ref, True)

    @pl.when(qi % 2 == 1)
    def _():
        scores(qi, sb_ref)
        absorb(qi - 1, sa_ref, False)
        absorb(qi, sb_ref, True)

    lp = lam_ref[...]
    lam = (jnp.exp(jnp.sum(lp[0:1] * lp[1:2], axis=-1, keepdims=True))
           - jnp.exp(jnp.sum(lp[2:3] * lp[3:4], axis=-1, keepdims=True)) + lambda_init)
    o = (_normalized(a1[...], DIFF_V_DIM) - lam * _normalized(a2[...], DIFF_V_DIM)).T
    o_ref[...] = (_rms(o) * g_ref[...] * (1.0 - lambda_init)).astype(BF16)


def _diff_attn(lam_p, qdt, kd, vdt, subln, lambda_init, t):
    b, s, _ = kd.shape
    nq = s // t
    kernel = functools.partial(_diff_attn_kernel, lambda_init=lambda_init, t=t)
    dv = DIFF_V_DIM + SUM_ROWS
    stat = pltpu.VMEM((1, t), F32)
    acc = pltpu.VMEM((dv, t), F32)
    score_buf = pltpu.VMEM((2, t, t), F32)
    return pl.pallas_call(
        kernel,
        out_shape=jax.ShapeDtypeStruct((b, s, DIFF_W), BF16),
        grid=(b, DIFF_HEADS, nq),
        in_specs=[
            pl.BlockSpec((4, DIFF_SUB_DIM), lambda i, h, j: (0, 0)),
            pl.BlockSpec((None, LANES, t), lambda i, h, j: (i * nq + j, h, 0)),
            pl.BlockSpec((None, s, LANES), lambda i, h, j: (i, 0, h)),
            pl.BlockSpec((nq, LANES, t), lambda i, h, j: (i, h, 0)),
            pl.BlockSpec((1, LANES), lambda i, h, j: (0, 0)),
        ],
        out_specs=pl.BlockSpec((None, t, LANES), lambda i, h, j: (i, j, h)),
        scratch_shapes=[stat, acc, stat, acc, score_buf, score_buf],
        compiler_params=_params(("parallel", "parallel", "arbitrary")),
        name="diff_attn",
    )(lam_p, qdt, kd, vdt, subln)


def _nsa_kernel(qpt_ref, qrt_ref, kc_ref, vct_ref, ks_ref, vst_ref, kw_ref, vwt_ref, ng_ref,
                ovlt_ref, o_ref, m_ref, acc_ref, part_ref, bias_ref, sa_ref, sb_ref,
                *, tq, tk, n_sel, top_n, wlen):
    r_heads = NSA_GROUP_SIZE
    hd = NSA_HEAD_DIM
    g = pl.program_id(1)
    q0 = pl.program_id(2) * tq
    qpos = q0 + lax.broadcasted_iota(jnp.int32, (1, tq), 1)

    def heads(x):
        return jnp.concatenate([x] * r_heads, axis=1)

    def head_cols(x, r):
        return x[:, r * tq:(r + 1) * tq]

    def stack_q(qt_ref):
        qt = qt_ref[...]
        q64 = jnp.concatenate([qt[r * hd:(r + 1) * hd, :] for r in range(r_heads)], axis=1)
        q128 = jnp.concatenate([q64, q64], axis=0)
        half = lax.broadcasted_iota(jnp.int32, q128.shape, 0) // hd
        return jnp.where(half == g, q128, jnp.zeros_like(q128))

    q_plain = stack_q(qpt_ref)
    q_rot = stack_q(qrt_ref)

    ncp = kc_ref.shape[0]
    n_idx = lax.broadcasted_iota(jnp.int32, (ncp, tq), 0)
    c_ok = (n_idx * CMP_STRIDE + (CMP_BLOCK - 1)) <= qpos
    s_c = _dot(kc_ref[...], q_plain) + heads(jnp.where(c_ok, 0.0, NEG_INF))
    e_c = jnp.exp2(s_c - jnp.max(s_c, axis=0, keepdims=True))
    p_c = e_c / jnp.sum(e_c, axis=0, keepdims=True) * heads(jnp.where(c_ok, 1.0, 0.0))
    o_c = _dot(vct_ref[0:hd, :], p_c.astype(BF16))

    p_sum = head_cols(p_c, 0)
    for r in range(1, r_heads):
        p_sum = p_sum + head_cols(p_c, r)
    ovlt = ovlt_ref[...]
    nb = ovlt.shape[0]
    imp, rem = None, p_sum
    for term in range(3):
        piece = rem.astype(BF16)
        d = _dot(ovlt, piece)
        imp = d if imp is None else imp + d
        rem = rem - piece.astype(F32)
    blk = lax.broadcasted_iota(jnp.int32, (nb, tq), 0)
    q_blk = qpos // SLC_BLOCK
    valid = blk <= q_blk
    forced = (blk == 0) | (blk == q_blk) | (blk == q_blk - 1)
    score = jnp.where(valid, imp + SLC_FORCED_BONUS * jnp.where(forced, 1.0, 0.0), -1.0)
    rank = jnp.zeros((nb, tq), F32)
    for jp in range(n_sel):
        other = score[jp:jp + 1, :]
        gt = jnp.where(other > score, 1.0, 0.0)
        ge = jnp.where(other >= score, 1.0, 0.0)
        rank = rank + jnp.where(blk > jp, ge, gt)
    bias_ref[...] = jnp.where(valid & (rank < float(top_n)), 0.0, NEG_INF)

    start = pl.multiple_of(jnp.maximum(q0 + tq - wlen, 0), LANES)
    kw = kw_ref[pl.ds(start, wlen), :]
    vwt = _with_sum_rows(jnp.concatenate([vwt_ref[start // LANES + j] for j in range(wlen // LANES)], axis=1))
    dist = qpos - (start + lax.broadcasted_iota(jnp.int32, (wlen, tq), 0))
    w_ok = (dist >= 0) & (dist < WINDOW)
    s_w = _dot(kw, q_rot) + heads(jnp.where(w_ok, 0.0, NEG_INF))
    e_w = jnp.exp2(s_w - jnp.max(s_w, axis=0, keepdims=True))
    o_w = _normalized(_dot(vwt, e_w.astype(BF16)), hd)

    gt_all = jax.nn.sigmoid(ng_ref[...]).T

    def gate(r, c):
        return gt_all[3 * r + c:3 * r + c + 1, :]

    for r in range(r_heads):
        part_ref[r * hd:(r + 1) * hd, :] = gate(r, 0) * head_cols(o_c, r) + gate(r, 2) * head_cols(o_w, r)

    m_ref[...] = jnp.full(m_ref.shape, NEG_INF, F32)
    acc_ref[...] = jnp.zeros(acc_ref.shape, F32)
    sub = tk // LANES
    blk_per_tile = tk // SLC_BLOCK

    def scores(kt, s_ref):
        s_ref[...] = _dot(ks_ref[pl.ds(pl.multiple_of(kt * tk, tk), tk), :], q_rot)

    def absorb(kt, s_ref, causal):
        vt = _with_sum_rows(jnp.concatenate([vst_ref[kt * sub + j] for j in range(sub)], axis=1))
        bias = jnp.concatenate(
            [jnp.broadcast_to(bias_ref[pl.ds(kt * blk_per_tile + i, 1), :], (SLC_BLOCK, tq))
             for i in range(blk_per_tile)], axis=0)
        if causal:
            kpos = kt * tk + lax.broadcasted_iota(jnp.int32, (tk, tq), 0)
            bias = jnp.where(kpos <= qpos, bias, NEG_INF)
        _online_update(s_ref[...] + heads(bias), vt, m_ref, acc_ref)

    kt_last = q0 // tk
    scores(0, sa_ref)

    def sel_body(j, carry):
        scores(2 * j + 1, sb_ref)
        absorb(2 * j, sa_ref, False)
        scores(2 * j + 2, sa_ref)
        absorb(2 * j + 1, sb_ref, False)
        return carry

    lax.fori_loop(0, kt_last // 2, sel_body, 0)

    @pl.when(kt_last % 2 == 0)
    def _():
        absorb(kt_last, sa_ref, True)

    @pl.when(kt_last % 2 == 1)
    def _():
        scores(kt_last, sb_ref)
        absorb(kt_last - 1, sa_ref, False)
        absorb(kt_last, sb_ref, True)

    o_s = _normalized(acc_ref[...], hd)
    outs = [part_ref[r * hd:(r + 1) * hd, :] + gate(r, 1) * head_cols(o_s, r) for r in range(r_heads)]
    o_ref[...] = jnp.concatenate(outs, axis=0).T.astype(BF16)


def _nsa_attn(nqpt, nqrt, cmp_kv, cmp_kvt, ks, vst, kw, vwt, z, ovlt, b, s, tk):
    tq = LANES
    n_sel = s // SLC_BLOCK
    top_n = min(SLC_TOP_N, n_sel)
    wlen = min(WINDOW + tq, s)
    ncp = cmp_kv.shape[3]
    gw = NSA_GROUP_SIZE * NSA_HEAD_DIM
    nq = s // tq
    kernel = functools.partial(_nsa_kernel, tq=tq, tk=tk, n_sel=n_sel, top_n=top_n, wlen=wlen)
    qt_spec = pl.BlockSpec((None, gw, tq), lambda i, g, j: (i * nq + j, g, 0))
    k_spec = pl.BlockSpec((s, LANES), lambda i, g, j: (i, 0))
    dv = NSA_HEAD_DIM + SUM_ROWS
    vt_spec = pl.BlockSpec((None, s // LANES, NSA_HEAD_DIM, LANES), lambda i, g, j: (g, i, 0, 0))
    cols = NSA_GROUP_SIZE * tq
    return pl.pallas_call(
        kernel,
        out_shape=jax.ShapeDtypeStruct((b, s, NSA_W), BF16),
        grid=(b, NSA_KV_GROUPS, nq),
        in_specs=[
            qt_spec, qt_spec,
            pl.BlockSpec((None, None, None, ncp, LANES), lambda i, g, j: (0, i, g, 0, 0)),
            pl.BlockSpec((None, None, None, LANES, ncp), lambda i, g, j: (1, i, g, 0, 0)),
            k_spec, vt_spec, k_spec, vt_spec,
            pl.BlockSpec((tq, LANES), lambda i, g, j: (i * nq + j, Z_NG // LANES + g)),
            pl.BlockSpec(ovlt.shape, lambda i, g, j: (0, 0)),
        ],
        out_specs=pl.BlockSpec((None, tq, gw), lambda i, g, j: (i, j, g)),
        scratch_shapes=[pltpu.VMEM((1, cols), F32),
                        pltpu.VMEM((dv, cols), F32), pltpu.VMEM((gw, tq), F32),
                        pltpu.VMEM((n_sel, tq), F32),
                        pltpu.VMEM((tk, cols), F32), pltpu.VMEM((tk, cols), F32)],
        compiler_params=_params(("parallel", "parallel", "arbitrary")),
        name="nsa_attn",
    )(nqpt, nqrt, cmp_kv, cmp_kvt, ks, vst, kw, vwt, z, ovlt)


def _merge_out_kernel(x_ref, ya_ref, yb_ref, ga_ref, gb_ref, wa_ref, wb_ref, wo_ref, o_ref, mg_ref,
                      *, tn):
    d = x_ref.shape[1]
    ya = ya_ref[...]
    yb = yb_ref[...]
    for c in range(0, d, tn):
        ta = _dot(ya, wa_ref[:, c:c + tn])
        tb = _dot(yb, wb_ref[:, c:c + tn])
        mg_ref[:, c:c + tn] = (jax.nn.sigmoid(ga_ref[:, c:c + tn]) * ta
                               + jax.nn.sigmoid(gb_ref[:, c:c + tn]) * tb).astype(BF16)
    mg = mg_ref[...]
    for c in range(0, d, tn):
        o_ref[:, c:c + tn] = x_ref[:, c:c + tn] + _dot(mg, wo_ref[:, c:c + tn])


def _merge_out(x2d, ya, yb, zg, wa, wb, wo, tm, tn):
    n, d = x2d.shape
    row = lambda i: (i, 0)
    const = lambda i: (0, 0)
    resident = pl.Buffered(1)
    return pl.pallas_call(
        functools.partial(_merge_out_kernel, tn=tn),
        out_shape=jax.ShapeDtypeStruct((n, d), F32),
        grid=(n // tm,),
        in_specs=[
            pl.BlockSpec((tm, d), row),
            pl.BlockSpec((tm, DIFF_W), row),
            pl.BlockSpec((tm, NSA_W), row),
            pl.BlockSpec((tm, d), lambda i: (i, 0)),
            pl.BlockSpec((tm, d), lambda i: (i, 1)),
            pl.BlockSpec((DIFF_W, d), const, pipeline_mode=resident),
            pl.BlockSpec((NSA_W, d), const, pipeline_mode=resident),
            pl.BlockSpec((d, d), const, pipeline_mode=resident),
        ],
        out_specs=pl.BlockSpec((tm, d), row),
        scratch_shapes=[pltpu.VMEM((tm, d), BF16)],
        compiler_params=_params(("parallel",)),
        name="merge_out",
    )(x2d, ya, yb, zg, zg, wa, wb, wo)


def _mlp_kernel(x_ref, g_ref, wu_ref, wd_ref, o_ref, h_ref, acc_ref):
    f = pl.program_id(1)

    @pl.when(f == 0)
    def _():
        h_ref[...] = (_rms(x_ref[...]) * g_ref[...]).astype(BF16)
        acc_ref[...] = jnp.zeros(acc_ref.shape, F32)

    u = jnp.maximum(_dot(h_ref[...], wu_ref[...]), 0.0)
    acc_ref[...] += _dot((u * u).astype(BF16), wd_ref[...])

    @pl.when(f == pl.num_programs(1) - 1)
    def _():
        o_ref[...] = x_ref[...] + acc_ref[...]


def _mlp(x2d, g, wu, wd, tm, tf):
    n, d = x2d.shape
    ff = wu.shape[1]
    return pl.pallas_call(
        _mlp_kernel,
        out_shape=jax.ShapeDtypeStruct((n, d), F32),
        grid=(n // tm, ff // tf),
        in_specs=[
            pl.BlockSpec((tm, d), lambda i, f: (i, 0)),
            pl.BlockSpec((1, d), lambda i, f: (0, 0)),
            pl.BlockSpec((d, tf), lambda i, f: (0, f)),
            pl.BlockSpec((tf, d), lambda i, f: (f, 0)),
        ],
        out_specs=pl.BlockSpec((tm, d), lambda i, f: (i, 0)),
        scratch_shapes=[pltpu.VMEM((tm, d), BF16), pltpu.VMEM((tm, d), F32)],
        compiler_params=_params(("parallel", "arbitrary")),
        name="mlp",
    )(x2d, g, wu, wd)


def _ple_kernel(x_ref, p_ref, g_ref, wp_ref, wg_ref, o_ref):
    x = x_ref[...]
    e = _rms(_dot(p_ref[...].astype(BF16), wp_ref[...])) * g_ref[...]
    gate = jax.nn.sigmoid(_dot(_rms(x).astype(BF16), wg_ref[...]))
    o_ref[...] = x + gate * e


def _ple(x2d, p2d, g, wp, wg, tm):
    n, d = x2d.shape
    row = lambda i: (i, 0)
    const = lambda i: (0, 0)
    resident = pl.Buffered(1)
    return pl.pallas_call(
        _ple_kernel,
        out_shape=jax.ShapeDtypeStruct((n, d), F32),
        grid=(n // tm,),
        in_specs=[
            pl.BlockSpec((tm, d), row),
            pl.BlockSpec((tm, PLE_DIM), row),
            pl.BlockSpec((1, d), const),
            pl.BlockSpec((PLE_DIM, d), const, pipeline_mode=resident),
            pl.BlockSpec((d, d), const, pipeline_mode=resident),
        ],
        out_specs=pl.BlockSpec((tm, d), row),
        compiler_params=_params(("parallel",)),
        name="ple",
    )(x2d, p2d, g, wp, wg)


def _rope_tables(positions):
    rot = 2 * ROT_HALF
    inv_freq = jnp.power(ROPE_THETA, -jnp.arange(0, rot, 2, dtype=F32) / rot)
    ang = positions.astype(F32)[..., None] * inv_freq
    cos, sin = jnp.cos(ang), jnp.sin(ang)
    n = cos.shape[0] * cos.shape[1]
    cos, sin = cos.reshape(n, ROT_HALF), sin.reshape(n, ROT_HALF)
    rest = NSA_HEAD_DIM - rot
    ones, zeros_r, zeros_h = jnp.ones((n, rest), F32), jnp.zeros((n, rest), F32), jnp.zeros((n, ROT_HALF), F32)
    ra = jnp.concatenate([cos, cos, ones], axis=1)
    rb = jnp.concatenate([-sin, zeros_h, zeros_r], axis=1)
    rc = jnp.concatenate([zeros_h, sin, zeros_r], axis=1)
    return tuple(jnp.tile(t, (1, LANES // NSA_HEAD_DIM)) for t in (ra, rb, rc))


def _regroup_w_in(w_in):
    d = w_in.shape[0]
    ng0 = 3 * DIFF_W + NSA_W + 6 * NSA_KV_W
    per_g = 3 * NSA_GROUP_SIZE
    parts = [w_in[:, :ng0]]
    for g in range(NSA_KV_GROUPS):
        parts += [w_in[:, ng0 + g * per_g:ng0 + (g + 1) * per_g], jnp.zeros((d, LANES - per_g), w_in.dtype)]
    w_main = jnp.concatenate(parts, axis=1).astype(BF16)
    w_gate = w_in[:, ng0 + 3 * NSA_HEADS:].astype(BF16)
    return w_main, w_gate


def _pick(n, pref):
    t = min(pref, n)
    while n % t:
        t //= 2
    return t


def kernel(x, p, positions, norm_mix, w_in, diff_q_norm, diff_k_norm, diff_lambda, diff_subln,
           nsa_q_norm, nsa_k_norm, cmp_pos, cmp_w1, cmp_w2, w_proj_diff, w_proj_nsa, w_out,
           norm_mlp, w_mlp_up, w_mlp_down, w_ple_proj, norm_ple, w_ple_gate):
    b, s, d = x.shape
    n = b * s
    assert d == D_MODEL and s % 512 == 0
    ra, rb, rc = _rope_tables(positions)
    seg = (jnp.arange(LANES)[:, None] // NSA_HEAD_DIM == jnp.arange(LANES)[None, :] // NSA_HEAD_DIM).astype(BF16)
    ncp = s // CMP_STRIDE
    n_sel = s // SLC_BLOCK
    assert n_sel % 16 == 0 and n_sel <= LANES
    cs = jnp.arange(ncp)[None, :] * CMP_STRIDE
    ss = jnp.arange(n_sel)[:, None] * SLC_BLOCK
    ovlt = ((cs < ss + SLC_BLOCK) & (cs + CMP_BLOCK > ss)).astype(BF16)
    tk_nsa, t_diff = 512, 512

    x2d = x.reshape(n, d)
    for i in range(p.shape[0]):
        lambda_init = 0.8 - 0.6 * math.exp(-0.3 * i)
        dup = lambda v: jnp.tile(v, LANES // NSA_HEAD_DIM)
        gains = jnp.zeros((8, LANES), F32).at[0].set(dup(diff_q_norm[i])).at[1].set(dup(diff_k_norm[i]))
        gains = gains.at[2].set(dup(nsa_q_norm[i])).at[3].set(dup(nsa_k_norm[i]))

        w_main, w_gate = _regroup_w_in(w_in[i])
        z = _in_proj(x2d, norm_mix[i][None], w_main, _pick(n, 1024), 512)
        zg = _in_proj(x2d, norm_mix[i][None], w_gate, _pick(n, 1024), 512)
        qdt, kd, vdt, nqpt, nqrt, ks, vst, kw, vwt = _prep(z, ra, rb, rc, gains, seg, _pick(t_diff, 256), t_diff)

        def chunks(col):
            a = z[:, col:col + NSA_KV_W].reshape(b, ncp, CMP_STRIDE, NSA_KV_GROUPS, NSA_HEAD_DIM)
            return a.transpose(0, 3, 1, 2, 4).reshape(b, NSA_KV_GROUPS, ncp, CMP_STRIDE * NSA_HEAD_DIM)

        half = CMP_STRIDE * NSA_HEAD_DIM
        cmp_kv, cmp_kvt = _compress(
            jnp.stack([chunks(Z_KC), chunks(Z_VC)]),
            cmp_pos[i].reshape(2, 2, half),
            cmp_w1[i].reshape(2, 2, half, CMP_HIDDEN).astype(BF16),
            jnp.tile(cmp_w2[i], (1, 1, LANES // NSA_HEAD_DIM)).astype(BF16),
            gains[3:4])

        ya = _diff_attn(diff_lambda[i], qdt, kd.reshape(b, s, DIFF_W), vdt, diff_subln[i][None],
                        lambda_init, t_diff)
        yb = _nsa_attn(nqpt, nqrt, cmp_kv, cmp_kvt, ks, vst, kw, vwt, z, ovlt, b, s, tk_nsa)

        x2d = _merge_out(x2d, ya.reshape(n, DIFF_W), yb.reshape(n, NSA_W), zg,
                         w_proj_diff[i].astype(BF16), w_proj_nsa[i].astype(BF16), w_out[i].astype(BF16),
                         _pick(n, 256), 512)
        x2d = _mlp(x2d, norm_mlp[i][None], w_mlp_up[i].astype(BF16), w_mlp_down[i].astype(BF16),
                   _pick(n, 512), 1024)
        x2d = _ple(x2d, p[i].reshape(n, PLE_DIM), norm_ple[i][None], w_ple_proj[i].astype(BF16),
                   w_ple_gate[i].astype(BF16), _pick(n, 256))
    return x2d.reshape(b, s, d)
```

```python
import functools
import math

import jax
import jax.numpy as jnp
from jax import lax
from jax.experimental import pallas as pl
from jax.experimental.pallas import tpu as pltpu

F32 = jnp.float32
BF16 = jnp.bfloat16

D_MODEL = 2048
PLE_DIM = 256
ROPE_THETA = 500000.0
ROPE_FRACTION = 4
NORM_EPS = 1e-6
NEG_INF = -1e30
DIFF_HEADS = 8
DIFF_SUB_DIM = 64
DIFF_V_DIM = 2 * DIFF_SUB_DIM
NSA_HEADS = 16
NSA_KV_GROUPS = 2
NSA_GROUP_SIZE = NSA_HEADS // NSA_KV_GROUPS
NSA_HEAD_DIM = 64
CMP_BLOCK = 32
CMP_STRIDE = 16
CMP_HIDDEN = 256
SLC_BLOCK = 64
SLC_TOP_N = 16
SLC_FORCED_BONUS = 1e4
WINDOW = 512
D_FF = 4 * D_MODEL
DIFF_W = DIFF_HEADS * DIFF_V_DIM
NSA_W = NSA_HEADS * NSA_HEAD_DIM
NSA_KV_W = NSA_KV_GROUPS * NSA_HEAD_DIM
ROT_HALF = NSA_HEAD_DIM // ROPE_FRACTION // 2
QK_SCALE = NSA_HEAD_DIM ** -0.5
Q_SCALE = QK_SCALE * math.log2(math.e)
SUM_ROWS = 16

LANES = 128
VMEM_LIMIT = 56 * 1024 * 1024

Z_DQ, Z_DK, Z_DV, Z_NQ = 0, 1024, 2048, 3072
Z_KC, Z_VC, Z_KS, Z_VS, Z_KW, Z_VW = 4096, 4224, 4352, 4480, 4608, 4736
PREP_W = 4864
Z_NG = PREP_W


def _params(sem, flags=None):
    return pltpu.CompilerParams(dimension_semantics=sem, vmem_limit_bytes=VMEM_LIMIT, flags=flags)


def _dot(a, b):
    return jnp.dot(a, b, preferred_element_type=F32)


def _dot_nt(a, b):
    return lax.dot_general(a, b, (((1,), (1,)), ((), ())), preferred_element_type=F32)


def _rms(x, eps=NORM_EPS):
    return x * lax.rsqrt(jnp.mean(x * x, axis=-1, keepdims=True) + eps)


def _split_dot(a, m_bf16, terms):
    acc = None
    rem = a
    for t in range(terms):
        piece = rem.astype(BF16)
        d = _dot(piece, m_bf16)
        acc = d if acc is None else acc + d
        if t + 1 < terms:
            rem = rem - piece.astype(F32)
    return acc


def _in_proj_kernel(x_ref, g_ref, w_ref, o_ref, h_ref):
    @pl.when(pl.program_id(1) == 0)
    def _():
        h_ref[...] = (_rms(x_ref[...]) * g_ref[...]).astype(BF16)

    o_ref[...] = _dot(h_ref[...], w_ref[...])


def _in_proj(x2d, g, w, tm, tn):
    n, d = x2d.shape
    nout = w.shape[1]
    return pl.pallas_call(
        _in_proj_kernel,
        out_shape=jax.ShapeDtypeStruct((n, nout), F32),
        grid=(n // tm, nout // tn),
        in_specs=[
            pl.BlockSpec((tm, d), lambda i, j: (i, 0)),
            pl.BlockSpec((1, d), lambda i, j: (0, 0)),
            pl.BlockSpec((d, tn), lambda i, j: (0, j)),
        ],
        out_specs=pl.BlockSpec((tm, tn), lambda i, j: (i, j)),
        scratch_shapes=[pltpu.VMEM((tm, d), BF16)],
        compiler_params=_params(("parallel", "arbitrary")),
        name="in_proj",
    )(x2d, g, w)


def _prep_kernel(z_ref, ra_ref, rb_ref, rc_ref, gain_ref, seg_ref,
                 qdt_ref, kd_ref, vdt_ref, nqpt_ref, nqrt_ref, ks_ref, vst_ref, kw_ref, vwt_ref, xc_ref,
                 stage_ref):
    ra, rb, rc = ra_ref[...], rb_ref[...], rc_ref[...]
    seg = seg_ref[...]
    tm = ra.shape[0]
    nt = tm // LANES

    def rope(y):
        return (y * ra + pltpu.roll(y, LANES - ROT_HALF, 1) * rb + pltpu.roll(y, ROT_HALF, 1) * rc)

    def head_norm(xb, gain):
        ss = _split_dot(xb * xb, seg, 2)
        return xb * lax.rsqrt(ss * (1.0 / NSA_HEAD_DIM) + NORM_EPS) * gain

    def zblk(col):
        return z_ref[:, col:col + LANES]

    for hb in range(DIFF_W // LANES):
        c = hb * LANES
        qd = rope(head_norm(zblk(Z_DQ + c), gain_ref[0:1, :])) * Q_SCALE
        qdt_ref[c:c + LANES, :] = qd.T.astype(BF16)
        kd_ref[:, c:c + LANES] = rope(head_norm(zblk(Z_DK + c), gain_ref[1:2, :])).astype(BF16)
        vdt_ref[c:c + LANES, :] = zblk(Z_DV + c).T.astype(BF16)
        yq = head_norm(zblk(Z_NQ + c), gain_ref[2:3, :])
        qpt = (yq * Q_SCALE).T.astype(BF16)
        qrt = (rope(yq) * Q_SCALE).T.astype(BF16)
        for j in range(nt):
            nqpt_ref[j, c:c + LANES, :] = qpt[:, j * LANES:(j + 1) * LANES]
            nqrt_ref[j, c:c + LANES, :] = qrt[:, j * LANES:(j + 1) * LANES]

    ks_ref[...] = rope(head_norm(zblk(Z_KS), gain_ref[3:4, :])).astype(BF16)
    kw_ref[...] = rope(head_norm(zblk(Z_KW), gain_ref[3:4, :])).astype(BF16)
    hd = NSA_HEAD_DIM
    for col, v_out in ((Z_VS, vst_ref), (Z_VW, vwt_ref)):
        vt = zblk(col).T.astype(BF16)
        for g in range(NSA_KV_GROUPS):
            for j in range(nt):
                v_out[g, j] = vt[g * hd:(g + 1) * hd, j * LANES:(j + 1) * LANES]

    nchunk = tm // CMP_STRIDE
    lo = lax.broadcasted_iota(jnp.int32, (nchunk, LANES), 1) < hd
    for kv, col in enumerate((Z_KC, Z_VC)):
        stage_ref[...] = zblk(col)
        for t in range(0, CMP_STRIDE, 2):
            a = stage_ref[pl.ds(t, nchunk, stride=CMP_STRIDE), :]
            b = stage_ref[pl.ds(t + 1, nchunk, stride=CMP_STRIDE), :]
            xc_ref[kv, 0, :, t * hd:(t + 2) * hd] = jnp.where(lo, a, pltpu.roll(b, hd, 1))
            xc_ref[kv, 1, :, t * hd:(t + 2) * hd] = jnp.where(lo, pltpu.roll(a, hd, 1), b)


def _prep(z, ra, rb, rc, gains, seg, tm, t_diff):
    n = z.shape[0]
    nt = tm // LANES
    per = t_diff // tm
    row = lambda i: (i, 0)
    tile_t = lambda i: (i // per, 0, i % per)
    diff_t = jax.ShapeDtypeStruct((n // t_diff, DIFF_W, t_diff), BF16)
    nsa_qt = jax.ShapeDtypeStruct((n // LANES, NSA_W, LANES), BF16)
    nsa_vt = jax.ShapeDtypeStruct((NSA_KV_GROUPS, n // LANES, NSA_HEAD_DIM, LANES), BF16)
    diff_t_spec = pl.BlockSpec((None, DIFF_W, tm), tile_t)
    nsa_qt_spec = pl.BlockSpec((nt, NSA_W, LANES), lambda i: (i, 0, 0))
    nsa_vt_spec = pl.BlockSpec((NSA_KV_GROUPS, nt, NSA_HEAD_DIM, LANES), lambda i: (0, i, 0, 0))
    chunk_w = CMP_STRIDE * NSA_HEAD_DIM
    chunks = jax.ShapeDtypeStruct((2, NSA_KV_GROUPS, n // CMP_STRIDE, chunk_w), F32)
    chunks_spec = pl.BlockSpec((2, NSA_KV_GROUPS, tm // CMP_STRIDE, chunk_w), lambda i: (0, 0, i, 0))
    return pl.pallas_call(
        _prep_kernel,
        out_shape=[diff_t, jax.ShapeDtypeStruct((n, DIFF_W), BF16), diff_t, nsa_qt, nsa_qt,
                   jax.ShapeDtypeStruct((n, LANES), BF16), nsa_vt,
                   jax.ShapeDtypeStruct((n, LANES), BF16), nsa_vt, chunks],
        grid=(n // tm,),
        in_specs=[
            pl.BlockSpec((tm, PREP_W), row),
            pl.BlockSpec((tm, LANES), row),
            pl.BlockSpec((tm, LANES), row),
            pl.BlockSpec((tm, LANES), row),
            pl.BlockSpec((8, LANES), lambda i: (0, 0)),
            pl.BlockSpec((LANES, LANES), lambda i: (0, 0)),
        ],
        out_specs=[diff_t_spec, pl.BlockSpec((tm, DIFF_W), row), diff_t_spec, nsa_qt_spec, nsa_qt_spec,
                   pl.BlockSpec((tm, LANES), row), nsa_vt_spec,
                   pl.BlockSpec((tm, LANES), row), nsa_vt_spec, chunks_spec],
        scratch_shapes=[pltpu.VMEM((tm, LANES), F32)],
        compiler_params=_params(("parallel",)),
        name="prep",
    )(z, ra, rb, rc, gains, seg)


def _compress_kernel(x_ref, pos_ref, w1_ref, w2_ref, gain_ref, o_ref, ot_ref):
    x = x_ref[...]
    nc = x.shape[0]
    a = (x + pos_ref[0:1, :]).astype(BF16)
    b = (x + pos_ref[1:2, :]).astype(BF16)
    u = _dot(a, w1_ref[0])
    v = _dot(b, w1_ref[1])
    pre = u + pltpu.roll(v, nc - 1, 0)
    cdf = 0.5 * (1.0 + jnp.tanh(math.sqrt(2.0 / math.pi) * (pre + 0.044715 * (pre * pre * pre))))
    c = _dot((pre * cdf).astype(BF16), w2_ref[...])
    is_key = pl.program_id(0) == 0
    out = jnp.where(is_key, _rms(c) * gain_ref[...], c)
    o_ref[...] = out.astype(BF16)
    ot_ref[...] = out.T.astype(BF16)


def _compress(xc, pos, w1, w2d, gain, b):
    _, g, n_chunks, cw = xc.shape
    nc = n_chunks // b
    return pl.pallas_call(
        _compress_kernel,
        out_shape=[jax.ShapeDtypeStruct((2, b, g, nc, LANES), BF16),
                   jax.ShapeDtypeStruct((2, b, g, LANES, nc), BF16)],
        grid=(2, b, g),
        in_specs=[
            pl.BlockSpec((None, None, nc, cw), lambda t, i, j: (t, j, i, 0)),
            pl.BlockSpec((None, 2, cw), lambda t, i, j: (t, 0, 0)),
            pl.BlockSpec((None, 2, cw, CMP_HIDDEN), lambda t, i, j: (t, 0, 0, 0)),
            pl.BlockSpec((None, CMP_HIDDEN, LANES), lambda t, i, j: (t, 0, 0)),
            pl.BlockSpec((1, LANES), lambda t, i, j: (0, 0)),
        ],
        out_specs=[pl.BlockSpec((None, None, None, nc, LANES), lambda t, i, j: (t, i, j, 0, 0)),
                   pl.BlockSpec((None, None, None, LANES, nc), lambda t, i, j: (t, i, j, 0, 0))],
        compiler_params=_params(("parallel", "parallel", "parallel")),
        name="compress",
    )(xc, pos, w1, w2d, gain)


def _online_update(s, vt, m_ref, acc_ref):
    m_old = m_ref[...]
    m_new = jnp.maximum(m_old, jnp.max(s, axis=0, keepdims=True))
    p = jnp.exp2(s - m_new)
    acc_ref[...] = jnp.exp2(m_old - m_new) * acc_ref[...] + _dot(vt, p.astype(BF16))
    m_ref[...] = m_new


def _with_sum_rows(vt):
    rows = lax.broadcasted_iota(jnp.int32, (SUM_ROWS, vt.shape[1]), 0)
    return jnp.concatenate([vt, jnp.where(rows == 0, 1.0, 0.0).astype(vt.dtype)], axis=0)


def _normalized(acc, dv):
    return acc[0:dv, :] / acc[dv:dv + 1, :]


def _diff_attn_kernel(lam_ref, qt_ref, k_ref, vt_ref, g_ref, o_ref,
                      m1, a1, m2, a2, sa_ref, sb_ref, *, lambda_init, t):
    qi = pl.program_id(2)
    qt = qt_ref[...]
    row = lax.broadcasted_iota(jnp.int32, qt.shape, 0)
    zero = jnp.zeros_like(qt)
    q_sub = (jnp.where(row < DIFF_SUB_DIM, qt, zero), jnp.where(row >= DIFF_SUB_DIM, qt, zero))
    stats = ((m1, a1), (m2, a2))
    for m_ref, a_ref in stats:
        m_ref[...] = jnp.full(m_ref.shape, NEG_INF, F32)
        a_ref[...] = jnp.zeros(a_ref.shape, F32)

    def scores(kt, s_ref):
        k = k_ref[pl.ds(pl.multiple_of(kt * t, t), t), :]
        for i in range(2):
            s_ref[i] = _dot(k, q_sub[i])

    def absorb(kt, s_ref, causal):
        vt = _with_sum_rows(vt_ref[kt])
        for i in range(2):
            s = s_ref[i]
            if causal:
                kpos = lax.broadcasted_iota(jnp.int32, s.shape, 0)
                qpos = lax.broadcasted_iota(jnp.int32, s.shape, 1)
                s = jnp.where(kpos <= qpos, s, NEG_INF)
            _online_update(s, vt, *stats[i])

    scores(0, sa_ref)

    def body(j, carry):
        scores(2 * j + 1, sb_ref)
        absorb(2 * j, sa_ref, False)
        scores(2 * j + 2, sa_ref)
        absorb(2 * j + 1, sb_ref, False)
        return carry

    lax.fori_loop(0, qi // 2, body, 0)

    @pl.when(qi % 2 == 0)
    def _():
        absorb(qi, sa_ref, True)

    @pl.when(qi % 2 == 1)
    def _():
        scores(qi, sb_ref)
        absorb(qi - 1, sa_ref, False)
        absorb(qi, sb_ref, True)

    lp = lam_ref[...]
    lam = (jnp.exp(jnp.sum(lp[0:1] * lp[1:2], axis=-1, keepdims=True))
           - jnp.exp(jnp.sum(lp[2:3] * lp[3:4], axis=-1, keepdims=True)) + lambda_init)
    o = (_normalized(a1[...], DIFF_V_DIM) - lam * _normalized(a2[...], DIFF_V_DIM)).T
    o_ref[...] = (_rms(o) * g_ref[...] * (1.0 - lambda_init)).astype(BF16)


def _diff_attn(lam_p, qdt, kd, vdt, subln, lambda_init, t):
    b, s, _ = kd.shape
    nq = s // t
    kernel = functools.partial(_diff_attn_kernel, lambda_init=lambda_init, t=t)
    dv = DIFF_V_DIM + SUM_ROWS
    stat = pltpu.VMEM((1, t), F32)
    acc = pltpu.VMEM((dv, t), F32)
    score_buf = pltpu.VMEM((2, t, t), F32)
    return pl.pallas_call(
        kernel,
        out_shape=jax.ShapeDtypeStruct((b, s, DIFF_W), BF16),
        grid=(b, DIFF_HEADS, nq),
        in_specs=[
            pl.BlockSpec((4, DIFF_SUB_DIM), lambda i, h, j: (0, 0)),
            pl.BlockSpec((None, LANES, t), lambda i, h, j: (i * nq + j, h, 0)),
            pl.BlockSpec((None, s, LANES), lambda i, h, j: (i, 0, h)),
            pl.BlockSpec((nq, LANES, t), lambda i, h, j: (i, h, 0)),
            pl.BlockSpec((1, LANES), lambda i, h, j: (0, 0)),
        ],
        out_specs=pl.BlockSpec((None, t, LANES), lambda i, h, j: (i, j, h)),
        scratch_shapes=[stat, acc, stat, acc, score_buf, score_buf],
        compiler_params=_params(("parallel", "parallel", "arbitrary")),
        name="diff_attn",
    )(lam_p, qdt, kd, vdt, subln)


def _nsa_kernel(qpt_ref, qrt_ref, kc_ref, vct_ref, ks_ref, vst_ref, kw_ref, vwt_ref, ng_ref,
                ovlt_ref, o_ref, m_ref, acc_ref, part_ref, bias_ref, sa_ref, sb_ref,
                *, tq, tk, n_sel, top_n, wlen):
    r_heads = NSA_GROUP_SIZE
    hd = NSA_HEAD_DIM
    g = pl.program_id(1)
    q0 = pl.program_id(2) * tq
    qpos = q0 + lax.broadcasted_iota(jnp.int32, (1, tq), 1)

    def heads(x):
        return jnp.concatenate([x] * r_heads, axis=1)

    def head_cols(x, r):
        return x[:, r * tq:(r + 1) * tq]

    def stack_q(qt_ref):
        qt = qt_ref[...]
        q64 = jnp.concatenate([qt[r * hd:(r + 1) * hd, :] for r in range(r_heads)], axis=1)
        q128 = jnp.concatenate([q64, q64], axis=0)
        half = lax.broadcasted_iota(jnp.int32, q128.shape, 0) // hd
        return jnp.where(half == g, q128, jnp.zeros_like(q128))

    q_plain = stack_q(qpt_ref)
    q_rot = stack_q(qrt_ref)

    def scores(kt, s_ref):
        s_ref[...] = _dot(ks_ref[pl.ds(pl.multiple_of(kt * tk, tk), tk), :], q_rot)

    ncp = kc_ref.shape[0]
    start = pl.multiple_of(jnp.maximum(q0 + tq - wlen, 0), LANES)
    raw_c = _dot(kc_ref[...], q_plain)
    raw_w = _dot(kw_ref[pl.ds(start, wlen), :], q_rot)
    scores(0, sa_ref)

    n_idx = lax.broadcasted_iota(jnp.int32, (ncp, tq), 0)
    c_ok = (n_idx * CMP_STRIDE + (CMP_BLOCK - 1)) <= qpos
    s_c = raw_c + heads(jnp.where(c_ok, 0.0, NEG_INF))
    e_c = jnp.exp2(s_c - jnp.max(s_c, axis=0, keepdims=True))
    p_c = e_c / jnp.sum(e_c, axis=0, keepdims=True) * heads(jnp.where(c_ok, 1.0, 0.0))
    o_c = _dot(vct_ref[0:hd, :], p_c.astype(BF16))

    p_sum = head_cols(p_c, 0)
    for r in range(1, r_heads):
        p_sum = p_sum + head_cols(p_c, r)
    ovlt = ovlt_ref[...]
    nb = ovlt.shape[0]
    imp, rem = None, p_sum
    for term in range(3):
        piece = rem.astype(BF16)
        d = _dot(ovlt, piece)
        imp = d if imp is None else imp + d
        rem = rem - piece.astype(F32)
    blk = lax.broadcasted_iota(jnp.int32, (nb, tq), 0)
    q_blk = qpos // SLC_BLOCK
    valid = blk <= q_blk
    forced = (blk == 0) | (blk == q_blk) | (blk == q_blk - 1)
    score = jnp.where(valid, imp + SLC_FORCED_BONUS * jnp.where(forced, 1.0, 0.0), -1.0)
    rank = jnp.zeros((nb, tq), F32)
    for jp in range(n_sel):
        other = score[jp:jp + 1, :]
        gt = jnp.where(other > score, 1.0, 0.0)
        ge = jnp.where(other >= score, 1.0, 0.0)
        rank = rank + jnp.where(blk > jp, ge, gt)
    bias_ref[...] = jnp.where(valid & (rank < float(top_n)), 0.0, NEG_INF)

    vwt = _with_sum_rows(jnp.concatenate([vwt_ref[start // LANES + j] for j in range(wlen // LANES)], axis=1))
    dist = qpos - (start + lax.broadcasted_iota(jnp.int32, (wlen, tq), 0))
    w_ok = (dist >= 0) & (dist < WINDOW)
    s_w = raw_w + heads(jnp.where(w_ok, 0.0, NEG_INF))
    e_w = jnp.exp2(s_w - jnp.max(s_w, axis=0, keepdims=True))
    o_w = _normalized(_dot(vwt, e_w.astype(BF16)), hd)

    gt_all = jax.nn.sigmoid(ng_ref[...]).T

    def gate(r, c):
        return gt_all[3 * r + c:3 * r + c + 1, :]

    for r in range(r_heads):
        part_ref[r * hd:(r + 1) * hd, :] = gate(r, 0) * head_cols(o_c, r) + gate(r, 2) * head_cols(o_w, r)

    m_ref[...] = jnp.full(m_ref.shape, NEG_INF, F32)
    acc_ref[...] = jnp.zeros(acc_ref.shape, F32)
    sub = tk // LANES
    blk_per_tile = tk // SLC_BLOCK

    def absorb(kt, s_ref, causal):
        vt = _with_sum_rows(jnp.concatenate([vst_ref[kt * sub + j] for j in range(sub)], axis=1))
        bias = jnp.concatenate(
            [jnp.broadcast_to(bias_ref[pl.ds(kt * blk_per_tile + i, 1), :], (SLC_BLOCK, tq))
             for i in range(blk_per_tile)], axis=0)
        if causal:
            kpos = kt * tk + lax.broadcasted_iota(jnp.int32, (tk, tq), 0)
            bias = jnp.where(kpos <= qpos, bias, NEG_INF)
        _online_update(s_ref[...] + heads(bias), vt, m_ref, acc_ref)

    kt_last = q0 // tk

    def sel_body(j, carry):
        scores(2 * j + 1, sb_ref)
        absorb(2 * j, sa_ref, False)
        scores(2 * j + 2, sa_ref)
        absorb(2 * j + 1, sb_ref, False)
        return carry

    lax.fori_loop(0, kt_last // 2, sel_body, 0)

    @pl.when(kt_last % 2 == 0)
    def _():
        absorb(kt_last, sa_ref, True)

    @pl.when(kt_last % 2 == 1)
    def _():
        scores(kt_last, sb_ref)
        absorb(kt_last - 1, sa_ref, False)
        absorb(kt_last, sb_ref, True)

    o_s = _normalized(acc_ref[...], hd)
    outs = [part_ref[r * hd:(r + 1) * hd, :] + gate(r, 1) * head_cols(o_s, r) for r in range(r_heads)]
    o_ref[...] = jnp.concatenate(outs, axis=0).T.astype(BF16)


def _nsa_attn(nqpt, nqrt, cmp_kv, cmp_kvt, ks, vst, kw, vwt, z, ovlt, b, s, tk):
    tq = LANES
    n_sel = s // SLC_BLOCK
    top_n = min(SLC_TOP_N, n_sel)
    wlen = min(WINDOW + tq, s)
    ncp = cmp_kv.shape[3]
    gw = NSA_GROUP_SIZE * NSA_HEAD_DIM
    nq = s // tq
    kernel = functools.partial(_nsa_kernel, tq=tq, tk=tk, n_sel=n_sel, top_n=top_n, wlen=wlen)
    qt_spec = pl.BlockSpec((None, gw, tq), lambda i, g, j: (i * nq + j, g, 0))
    k_spec = pl.BlockSpec((s, LANES), lambda i, g, j: (i, 0))
    dv = NSA_HEAD_DIM + SUM_ROWS
    vt_spec = pl.BlockSpec((None, s // LANES, NSA_HEAD_DIM, LANES), lambda i, g, j: (g, i, 0, 0))
    cols = NSA_GROUP_SIZE * tq
    return pl.pallas_call(
        kernel,
        out_shape=jax.ShapeDtypeStruct((b, s, NSA_W), BF16),
        grid=(b, NSA_KV_GROUPS, nq),
        in_specs=[
            qt_spec, qt_spec,
            pl.BlockSpec((None, None, None, ncp, LANES), lambda i, g, j: (0, i, g, 0, 0)),
            pl.BlockSpec((None, None, None, LANES, ncp), lambda i, g, j: (1, i, g, 0, 0)),
            k_spec, vt_spec, k_spec, vt_spec,
            pl.BlockSpec((tq, LANES), lambda i, g, j: (i * nq + j, Z_NG // LANES + g)),
            pl.BlockSpec(ovlt.shape, lambda i, g, j: (0, 0)),
        ],
        out_specs=pl.BlockSpec((None, tq, gw), lambda i, g, j: (i, j, g)),
        scratch_shapes=[pltpu.VMEM((1, cols), F32),
                        pltpu.VMEM((dv, cols), F32), pltpu.VMEM((gw, tq), F32),
                        pltpu.VMEM((n_sel, tq), F32),
                        pltpu.VMEM((tk, cols), F32), pltpu.VMEM((tk, cols), F32)],
        compiler_params=_params(("parallel", "parallel", "arbitrary")),
        name="nsa_attn",
    )(nqpt, nqrt, cmp_kv, cmp_kvt, ks, vst, kw, vwt, z, ovlt)


def _merge_out_kernel(x_ref, ya_ref, yb_ref, ga_ref, gb_ref, wa_ref, wb_ref, wo_ref, o_ref, mg_ref,
                      *, tn):
    d = x_ref.shape[1]
    ya = ya_ref[...]
    yb = yb_ref[...]
    for c in range(0, d, tn):
        ta = _dot(ya, wa_ref[:, c:c + tn])
        tb = _dot(yb, wb_ref[:, c:c + tn])
        mg_ref[:, c:c + tn] = (jax.nn.sigmoid(ga_ref[:, c:c + tn]) * ta
                               + jax.nn.sigmoid(gb_ref[:, c:c + tn]) * tb).astype(BF16)
    mg = mg_ref[...]
    for c in range(0, d, tn):
        o_ref[:, c:c + tn] = x_ref[:, c:c + tn] + _dot(mg, wo_ref[:, c:c + tn])


def _merge_out(x2d, ya, yb, zg, wa, wb, wo, tm, tn):
    n, d = x2d.shape
    row = lambda i: (i, 0)
    const = lambda i: (0, 0)
    resident = pl.Buffered(1)
    return pl.pallas_call(
        functools.partial(_merge_out_kernel, tn=tn),
        out_shape=jax.ShapeDtypeStruct((n, d), F32),
        grid=(n // tm,),
        in_specs=[
            pl.BlockSpec((tm, d), row),
            pl.BlockSpec((tm, DIFF_W), row),
            pl.BlockSpec((tm, NSA_W), row),
            pl.BlockSpec((tm, d), lambda i: (i, 0)),
            pl.BlockSpec((tm, d), lambda i: (i, 1)),
            pl.BlockSpec((DIFF_W, d), const, pipeline_mode=resident),
            pl.BlockSpec((NSA_W, d), const, pipeline_mode=resident),
            pl.BlockSpec((d, d), const, pipeline_mode=resident),
        ],
        out_specs=pl.BlockSpec((tm, d), row),
        scratch_shapes=[pltpu.VMEM((tm, d), BF16)],
        compiler_params=_params(("parallel",)),
        name="merge_out",
    )(x2d, ya, yb, zg, zg, wa, wb, wo)


def _mlp_kernel(x_ref, g_ref, wu_ref, wd_ref, o_ref, h_ref, acc_ref):
    f = pl.program_id(1)

    @pl.when(f == 0)
    def _():
        h_ref[...] = (_rms(x_ref[...]) * g_ref[...]).astype(BF16)
        acc_ref[...] = jnp.zeros(acc_ref.shape, F32)

    u = jnp.maximum(_dot(h_ref[...], wu_ref[...]), 0.0)
    acc_ref[...] += _dot((u * u).astype(BF16), wd_ref[...])

    @pl.when(f == pl.num_programs(1) - 1)
    def _():
        o_ref[...] = x_ref[...] + acc_ref[...]


def _mlp(x2d, g, wu, wd, tm, tf):
    n, d = x2d.shape
    ff = wu.shape[1]
    return pl.pallas_call(
        _mlp_kernel,
        out_shape=jax.ShapeDtypeStruct((n, d), F32),
        grid=(n // tm, ff // tf),
        in_specs=[
            pl.BlockSpec((tm, d), lambda i, f: (i, 0)),
            pl.BlockSpec((1, d), lambda i, f: (0, 0)),
            pl.BlockSpec((d, tf), lambda i, f: (0, f)),
            pl.BlockSpec((tf, d), lambda i, f: (f, 0)),
        ],
        out_specs=pl.BlockSpec((tm, d), lambda i, f: (i, 0)),
        scratch_shapes=[pltpu.VMEM((tm, d), BF16), pltpu.VMEM((tm, d), F32)],
        compiler_params=_params(("parallel", "arbitrary")),
        name="mlp",
    )(x2d, g, wu, wd)


def _ple_kernel(x_ref, p_ref, g_ref, wp_ref, wg_ref, o_ref):
    x = x_ref[...]
    e = _rms(_dot(p_ref[...].astype(BF16), wp_ref[...])) * g_ref[...]
    gate = jax.nn.sigmoid(_dot(_rms(x).astype(BF16), wg_ref[...]))
    o_ref[...] = x + gate * e


def _ple(x2d, p2d, g, wp, wg, tm):
    n, d = x2d.shape
    row = lambda i: (i, 0)
    const = lambda i: (0, 0)
    resident = pl.Buffered(1)
    return pl.pallas_call(
        _ple_kernel,
        out_shape=jax.ShapeDtypeStruct((n, d), F32),
        grid=(n // tm,),
        in_specs=[
            pl.BlockSpec((tm, d), row),
            pl.BlockSpec((tm, PLE_DIM), row),
            pl.BlockSpec((1, d), const),
            pl.BlockSpec((PLE_DIM, d), const, pipeline_mode=resident),
            pl.BlockSpec((d, d), const, pipeline_mode=resident),
        ],
        out_specs=pl.BlockSpec((tm, d), row),
        compiler_params=_params(("parallel",)),
        name="ple",
    )(x2d, p2d, g, wp, wg)


def _rope_tables(positions):
    rot = 2 * ROT_HALF
    inv_freq = jnp.power(ROPE_THETA, -jnp.arange(0, rot, 2, dtype=F32) / rot)
    ang = positions.astype(F32)[..., None] * inv_freq
    cos, sin = jnp.cos(ang), jnp.sin(ang)
    n = cos.shape[0] * cos.shape[1]
    cos, sin = cos.reshape(n, ROT_HALF), sin.reshape(n, ROT_HALF)
    rest = NSA_HEAD_DIM - rot
    ones, zeros_r, zeros_h = jnp.ones((n, rest), F32), jnp.zeros((n, rest), F32), jnp.zeros((n, ROT_HALF), F32)
    ra = jnp.concatenate([cos, cos, ones], axis=1)
    rb = jnp.concatenate([-sin, zeros_h, zeros_r], axis=1)
    rc = jnp.concatenate([zeros_h, sin, zeros_r], axis=1)
    return tuple(jnp.tile(t, (1, LANES // NSA_HEAD_DIM)) for t in (ra, rb, rc))


def _regroup_w_in(w_in):
    d = w_in.shape[0]
    ng0 = 3 * DIFF_W + NSA_W + 6 * NSA_KV_W
    per_g = 3 * NSA_GROUP_SIZE
    parts = [w_in[:, :ng0]]
    for g in range(NSA_KV_GROUPS):
        parts += [w_in[:, ng0 + g * per_g:ng0 + (g + 1) * per_g], jnp.zeros((d, LANES - per_g), w_in.dtype)]
    w_main = jnp.concatenate(parts, axis=1).astype(BF16)
    w_gate = w_in[:, ng0 + 3 * NSA_HEADS:].astype(BF16)
    return w_main, w_gate


def _pick(n, pref):
    t = min(pref, n)
    while n % t:
        t //= 2
    return t


def kernel(x, p, positions, norm_mix, w_in, diff_q_norm, diff_k_norm, diff_lambda, diff_subln,
           nsa_q_norm, nsa_k_norm, cmp_pos, cmp_w1, cmp_w2, w_proj_diff, w_proj_nsa, w_out,
           norm_mlp, w_mlp_up, w_mlp_down, w_ple_proj, norm_ple, w_ple_gate):
    b, s, d = x.shape
    n = b * s
    assert d == D_MODEL and s % 512 == 0
    ra, rb, rc = _rope_tables(positions)
    seg = (jnp.arange(LANES)[:, None] // NSA_HEAD_DIM == jnp.arange(LANES)[None, :] // NSA_HEAD_DIM).astype(BF16)
    ncp = s // CMP_STRIDE
    n_sel = s // SLC_BLOCK
    assert n_sel % 16 == 0 and n_sel <= LANES
    cs = jnp.arange(ncp)[None, :] * CMP_STRIDE
    ss = jnp.arange(n_sel)[:, None] * SLC_BLOCK
    ovlt = ((cs < ss + SLC_BLOCK) & (cs + CMP_BLOCK > ss)).astype(BF16)
    tk_nsa, t_diff = 512, 512

    x2d = x.reshape(n, d)
    for i in range(p.shape[0]):
        lambda_init = 0.8 - 0.6 * math.exp(-0.3 * i)
        dup = lambda v: jnp.tile(v, LANES // NSA_HEAD_DIM)
        gains = jnp.zeros((8, LANES), F32).at[0].set(dup(diff_q_norm[i])).at[1].set(dup(diff_k_norm[i]))
        gains = gains.at[2].set(dup(nsa_q_norm[i])).at[3].set(dup(nsa_k_norm[i]))

        w_main, w_gate = _regroup_w_in(w_in[i])
        z = _in_proj(x2d, norm_mix[i][None], w_main, _pick(n, 1024), 1024)
        zg = _in_proj(x2d, norm_mix[i][None], w_gate, _pick(n, 1024), 1024)
        qdt, kd, vdt, nqpt, nqrt, ks, vst, kw, vwt, xc = _prep(z, ra, rb, rc, gains, seg,
                                                               _pick(t_diff, 256), t_diff)

        half = CMP_STRIDE * NSA_HEAD_DIM
        cmp_kv, cmp_kvt = _compress(
            xc,
            cmp_pos[i].reshape(2, 2, half),
            cmp_w1[i].reshape(2, 2, half, CMP_HIDDEN).astype(BF16),
            jnp.tile(cmp_w2[i], (1, 1, LANES // NSA_HEAD_DIM)).astype(BF16),
            gains[3:4], b)

        ya = _diff_attn(diff_lambda[i], qdt, kd.reshape(b, s, DIFF_W), vdt, diff_subln[i][None],
                        lambda_init, t_diff)
        yb = _nsa_attn(nqpt, nqrt, cmp_kv, cmp_kvt, ks, vst, kw, vwt, z, ovlt, b, s, tk_nsa)

        x2d = _merge_out(x2d, ya.reshape(n, DIFF_W), yb.reshape(n, NSA_W), zg,
                         w_proj_diff[i].astype(BF16), w_proj_nsa[i].astype(BF16), w_out[i].astype(BF16),
                         _pick(n, 256), 512)
        x2d = _mlp(x2d, norm_mlp[i][None], w_mlp_up[i].astype(BF16), w_mlp_down[i].astype(BF16),
                   _pick(n, 512), 1024)
        x2d = _ple(x2d, p[i].reshape(n, PLE_DIM), norm_ple[i][None], w_ple_proj[i].astype(BF16),
                   w_ple_gate[i].astype(BF16), _pick(n, 512))
    return x2d.reshape(b, s, d)
```

```python
import functools
import math

import jax
import jax.numpy as jnp
from jax import lax
from jax.experimental import pallas as pl
from jax.experimental.pallas import tpu as pltpu

F32 = jnp.float32
BF16 = jnp.bfloat16

D_MODEL = 2048
PLE_DIM = 256
ROPE_THETA = 500000.0
ROPE_FRACTION = 4
NORM_EPS = 1e-6
NEG_INF = -1e30
DIFF_HEADS = 8
DIFF_SUB_DIM = 64
DIFF_V_DIM = 2 * DIFF_SUB_DIM
NSA_HEADS = 16
NSA_KV_GROUPS = 2
NSA_GROUP_SIZE = NSA_HEADS // NSA_KV_GROUPS
NSA_HEAD_DIM = 64
CMP_BLOCK = 32
CMP_STRIDE = 16
CMP_HIDDEN = 256
SLC_BLOCK = 64
SLC_TOP_N = 16
SLC_FORCED_BONUS = 1e4
WINDOW = 512
D_FF = 4 * D_MODEL
DIFF_W = DIFF_HEADS * DIFF_V_DIM
NSA_W = NSA_HEADS * NSA_HEAD_DIM
NSA_KV_W = NSA_KV_GROUPS * NSA_HEAD_DIM
ROT_HALF = NSA_HEAD_DIM // ROPE_FRACTION // 2
QK_SCALE = NSA_HEAD_DIM ** -0.5
Q_SCALE = QK_SCALE * math.log2(math.e)
SUM_ROWS = 16

LANES = 128
VMEM_LIMIT = 56 * 1024 * 1024


def _params(sem, flags=None):
    return pltpu.CompilerParams(dimension_semantics=sem, vmem_limit_bytes=VMEM_LIMIT, flags=flags)


def _dot(a, b):
    return jnp.dot(a, b, preferred_element_type=F32)


def _dot_nt(a, b):
    return lax.dot_general(a, b, (((1,), (1,)), ((), ())), preferred_element_type=F32)


def _rms(x, eps=NORM_EPS):
    return x * lax.rsqrt(jnp.mean(x * x, axis=-1, keepdims=True) + eps)


def _split_dot(a, m_bf16, terms):
    acc = None
    rem = a
    for t in range(terms):
        piece = rem.astype(BF16)
        d = _dot(piece, m_bf16)
        acc = d if acc is None else acc + d
        if t + 1 < terms:
            rem = rem - piece.astype(F32)
    return acc


def _in_proj_kernel(x_ref, g_ref, w_ref, o_ref, h_ref):
    @pl.when(pl.program_id(1) == 0)
    def _():
        h_ref[...] = (_rms(x_ref[...]) * g_ref[...]).astype(BF16)

    o_ref[...] = _dot(h_ref[...], w_ref[...])


def _in_proj(x2d, g, w, tm, tn):
    n, d = x2d.shape
    nout = w.shape[1]
    return pl.pallas_call(
        _in_proj_kernel,
        out_shape=jax.ShapeDtypeStruct((n, nout), F32),
        grid=(n // tm, nout // tn),
        in_specs=[
            pl.BlockSpec((tm, d), lambda i, j: (i, 0)),
            pl.BlockSpec((1, d), lambda i, j: (0, 0)),
            pl.BlockSpec((d, tn), lambda i, j: (0, j)),
        ],
        out_specs=pl.BlockSpec((tm, tn), lambda i, j: (i, j)),
        scratch_shapes=[pltpu.VMEM((tm, d), BF16)],
        compiler_params=_params(("parallel", "arbitrary")),
        name="in_proj",
    )(x2d, g, w)


def _head_ops(ra_ref, rb_ref, rc_ref, seg_ref):
    ra, rb, rc = ra_ref[...], rb_ref[...], rc_ref[...]
    seg = seg_ref[...]

    def rope(y):
        return (y * ra + pltpu.roll(y, LANES - ROT_HALF, 1) * rb + pltpu.roll(y, ROT_HALF, 1) * rc)

    def head_norm(xb, gain):
        ss = _split_dot(xb * xb, seg, 2)
        return xb * lax.rsqrt(ss * (1.0 / NSA_HEAD_DIM) + NORM_EPS) * gain

    return rope, head_norm


def _normed_matmul(x_ref, g_ref, w_ref, h_ref):
    @pl.when(pl.program_id(1) == 0)
    def _():
        h_ref[...] = (_rms(x_ref[...]) * g_ref[...]).astype(BF16)

    return _dot(h_ref[...], w_ref[...])


def _proj_diff_kernel(x_ref, g_ref, w_ref, ra_ref, rb_ref, rc_ref, gain_ref, seg_ref,
                      qdt_ref, kd_ref, vdt_ref, h_ref, *, tn, t_diff):
    acc = _normed_matmul(x_ref, g_ref, w_ref, h_ref)
    rope, head_norm = _head_ops(ra_ref, rb_ref, rc_ref, seg_ref)
    tm = acc.shape[0]
    per_group = DIFF_W // tn

    def store_t(out_ref, col, val):
        vt = val.T.astype(BF16)
        for t in range(tm // t_diff):
            out_ref[t, col:col + LANES, :] = vt[:, t * t_diff:(t + 1) * t_diff]

    for j in range(3 * per_group):
        @pl.when(pl.program_id(1) == j)
        def _(j=j):
            for c in range(0, tn, LANES):
                blk = acc[:, c:c + LANES]
                col = (j % per_group) * tn + c
                if j < per_group:
                    store_t(qdt_ref, col, rope(head_norm(blk, gain_ref[0:1, :])) * Q_SCALE)
                elif j < 2 * per_group:
                    kd_ref[:, col:col + LANES] = rope(head_norm(blk, gain_ref[1:2, :])).astype(BF16)
                else:
                    store_t(vdt_ref, col, blk)


def _proj_diff(x2d, g, w, ra, rb, rc, gains, seg, tm, tn, t_diff):
    n, d = x2d.shape
    row = lambda i, j: (i, 0)
    const = lambda i, j: (0, 0)
    diff_t = jax.ShapeDtypeStruct((n // t_diff, DIFF_W, t_diff), BF16)
    diff_t_spec = pl.BlockSpec((tm // t_diff, DIFF_W, t_diff), lambda i, j: (i, 0, 0))
    return pl.pallas_call(
        functools.partial(_proj_diff_kernel, tn=tn, t_diff=t_diff),
        out_shape=[diff_t, jax.ShapeDtypeStruct((n, DIFF_W), BF16), diff_t],
        grid=(n // tm, 3 * DIFF_W // tn),
        in_specs=[
            pl.BlockSpec((tm, d), row),
            pl.BlockSpec((1, d), const),
            pl.BlockSpec((d, tn), lambda i, j: (0, j)),
            pl.BlockSpec((tm, LANES), row),
            pl.BlockSpec((tm, LANES), row),
            pl.BlockSpec((tm, LANES), row),
            pl.BlockSpec((8, LANES), const),
            pl.BlockSpec((LANES, LANES), const),
        ],
        out_specs=[diff_t_spec, pl.BlockSpec((tm, DIFF_W), row), diff_t_spec],
        scratch_shapes=[pltpu.VMEM((tm, d), BF16)],
        compiler_params=_params(("parallel", "arbitrary")),
        name="proj_diff",
    )(x2d, g, w, ra, rb, rc, gains, seg)


def _proj_nsa_kernel(x_ref, g_ref, w_ref, ra_ref, rb_ref, rc_ref, gain_ref, seg_ref,
                     nqpt_ref, nqrt_ref, ks_ref, vst_ref, kw_ref, vwt_ref, xc_ref, ng_ref,
                     h_ref, stage_ref, *, tn):
    acc = _normed_matmul(x_ref, g_ref, w_ref, h_ref)
    rope, head_norm = _head_ops(ra_ref, rb_ref, rc_ref, seg_ref)
    tm = acc.shape[0]
    nt = tm // LANES
    hd = NSA_HEAD_DIM
    nq_tiles = NSA_W // tn
    assert tn == 4 * LANES

    def blk(c):
        return acc[:, c * LANES:(c + 1) * LANES]

    def store_vt(v_out, val):
        vt = val.T.astype(BF16)
        for g in range(NSA_KV_GROUPS):
            for t in range(nt):
                v_out[g, t] = vt[g * hd:(g + 1) * hd, t * LANES:(t + 1) * LANES]

    def store_chunks(kv, val):
        nchunk = tm // CMP_STRIDE
        lo = lax.broadcasted_iota(jnp.int32, (nchunk, LANES), 1) < hd
        stage_ref[...] = val
        for t in range(0, CMP_STRIDE, 2):
            a = stage_ref[pl.ds(t, nchunk, stride=CMP_STRIDE), :]
            b = stage_ref[pl.ds(t + 1, nchunk, stride=CMP_STRIDE), :]
            xc_ref[kv, 0, :, t * hd:(t + 2) * hd] = jnp.where(lo, a, pltpu.roll(b, hd, 1))
            xc_ref[kv, 1, :, t * hd:(t + 2) * hd] = jnp.where(lo, pltpu.roll(a, hd, 1), b)

    for j in range(nq_tiles):
        @pl.when(pl.program_id(1) == j)
        def _(j=j):
            for c in range(tn // LANES):
                col = j * tn + c * LANES
                yq = head_norm(blk(c), gain_ref[2:3, :])
                qpt = (yq * Q_SCALE).T.astype(BF16)
                qrt = (rope(yq) * Q_SCALE).T.astype(BF16)
                for t in range(nt):
                    nqpt_ref[t, col:col + LANES, :] = qpt[:, t * LANES:(t + 1) * LANES]
                    nqrt_ref[t, col:col + LANES, :] = qrt[:, t * LANES:(t + 1) * LANES]

    @pl.when(pl.program_id(1) == nq_tiles)
    def _():
        store_chunks(0, blk(0))
        store_chunks(1, blk(1))
        ks_ref[...] = rope(head_norm(blk(2), gain_ref[3:4, :])).astype(BF16)
        store_vt(vst_ref, blk(3))

    @pl.when(pl.program_id(1) == nq_tiles + 1)
    def _():
        kw_ref[...] = rope(head_norm(blk(0), gain_ref[3:4, :])).astype(BF16)
        store_vt(vwt_ref, blk(1))
        ng_ref[...] = acc[:, 2 * LANES:4 * LANES]


def _proj_nsa(x2d, g, w, ra, rb, rc, gains, seg, tm, tn):
    n, d = x2d.shape
    nt = tm // LANES
    row = lambda i, j: (i, 0)
    const = lambda i, j: (0, 0)
    nsa_qt = jax.ShapeDtypeStruct((n // LANES, NSA_W, LANES), BF16)
    nsa_vt = jax.ShapeDtypeStruct((NSA_KV_GROUPS, n // LANES, NSA_HEAD_DIM, LANES), BF16)
    key = jax.ShapeDtypeStruct((n, LANES), BF16)
    chunk_w = CMP_STRIDE * NSA_HEAD_DIM
    chunks = jax.ShapeDtypeStruct((2, NSA_KV_GROUPS, n // CMP_STRIDE, chunk_w), F32)
    gates = jax.ShapeDtypeStruct((n, NSA_KV_GROUPS * LANES), F32)
    nsa_qt_spec = pl.BlockSpec((nt, NSA_W, LANES), lambda i, j: (i, 0, 0))
    nsa_vt_spec = pl.BlockSpec((NSA_KV_GROUPS, nt, NSA_HEAD_DIM, LANES), lambda i, j: (0, i, 0, 0))
    key_spec = pl.BlockSpec((tm, LANES), row)
    chunks_spec = pl.BlockSpec((2, NSA_KV_GROUPS, tm // CMP_STRIDE, chunk_w), lambda i, j: (0, 0, i, 0))
    return pl.pallas_call(
        functools.partial(_proj_nsa_kernel, tn=tn),
        out_shape=[nsa_qt, nsa_qt, key, nsa_vt, key, nsa_vt, chunks, gates],
        grid=(n // tm, w.shape[1] // tn),
        in_specs=[
            pl.BlockSpec((tm, d), row),
            pl.BlockSpec((1, d), const),
            pl.BlockSpec((d, tn), lambda i, j: (0, j)),
            pl.BlockSpec((tm, LANES), row),
            pl.BlockSpec((tm, LANES), row),
            pl.BlockSpec((tm, LANES), row),
            pl.BlockSpec((8, LANES), const),
            pl.BlockSpec((LANES, LANES), const),
        ],
        out_specs=[nsa_qt_spec, nsa_qt_spec, key_spec, nsa_vt_spec, key_spec, nsa_vt_spec, chunks_spec,
                   pl.BlockSpec((tm, NSA_KV_GROUPS * LANES), row)],
        scratch_shapes=[pltpu.VMEM((tm, d), BF16), pltpu.VMEM((tm, LANES), F32)],
        compiler_params=_params(("parallel", "arbitrary")),
        name="proj_nsa",
    )(x2d, g, w, ra, rb, rc, gains, seg)


def _compress_kernel(x_ref, pos_ref, w1_ref, w2_ref, gain_ref, o_ref, ot_ref):
    x = x_ref[...]
    nc = x.shape[0]
    a = (x + pos_ref[0:1, :]).astype(BF16)
    b = (x + pos_ref[1:2, :]).astype(BF16)
    u = _dot(a, w1_ref[0])
    v = _dot(b, w1_ref[1])
    pre = u + pltpu.roll(v, nc - 1, 0)
    cdf = 0.5 * (1.0 + jnp.tanh(math.sqrt(2.0 / math.pi) * (pre + 0.044715 * (pre * pre * pre))))
    c = _dot((pre * cdf).astype(BF16), w2_ref[...])
    is_key = pl.program_id(0) == 0
    out = jnp.where(is_key, _rms(c) * gain_ref[...], c)
    o_ref[...] = out.astype(BF16)
    ot_ref[...] = out.T.astype(BF16)


def _compress(xc, pos, w1, w2d, gain, b):
    _, g, n_chunks, cw = xc.shape
    nc = n_chunks // b
    return pl.pallas_call(
        _compress_kernel,
        out_shape=[jax.ShapeDtypeStruct((2, b, g, nc, LANES), BF16),
                   jax.ShapeDtypeStruct((2, b, g, LANES, nc), BF16)],
        grid=(2, b, g),
        in_specs=[
            pl.BlockSpec((None, None, nc, cw), lambda t, i, j: (t, j, i, 0)),
            pl.BlockSpec((None, 2, cw), lambda t, i, j: (t, 0, 0)),
            pl.BlockSpec((None, 2, cw, CMP_HIDDEN), lambda t, i, j: (t, 0, 0, 0)),
            pl.BlockSpec((None, CMP_HIDDEN, LANES), lambda t, i, j: (t, 0, 0)),
            pl.BlockSpec((1, LANES), lambda t, i, j: (0, 0)),
        ],
        out_specs=[pl.BlockSpec((None, None, None, nc, LANES), lambda t, i, j: (t, i, j, 0, 0)),
                   pl.BlockSpec((None, None, None, LANES, nc), lambda t, i, j: (t, i, j, 0, 0))],
        compiler_params=_params(("parallel", "parallel", "parallel")),
        name="compress",
    )(xc, pos, w1, w2d, gain)


def _online_update(s, vt, m_ref, acc_ref):
    m_old = m_ref[...]
    m_new = jnp.maximum(m_old, jnp.max(s, axis=0, keepdims=True))
    p = jnp.exp2(s - m_new)
    acc_ref[...] = jnp.exp2(m_old - m_new) * acc_ref[...] + _dot(vt, p.astype(BF16))
    m_ref[...] = m_new


def _with_sum_rows(vt):
    rows = lax.broadcasted_iota(jnp.int32, (SUM_ROWS, vt.shape[1]), 0)
    return jnp.concatenate([vt, jnp.where(rows == 0, 1.0, 0.0).astype(vt.dtype)], axis=0)


def _normalized(acc, dv):
    return acc[0:dv, :] / acc[dv:dv + 1, :]


def _diff_attn_kernel(lam_ref, qt_ref, k_ref, vt_ref, g_ref, o_ref,
                      m1, a1, m2, a2, sa_ref, sb_ref, *, lambda_init, t):
    qi = pl.program_id(2)
    qt = qt_ref[...]
    row = lax.broadcasted_iota(jnp.int32, qt.shape, 0)
    zero = jnp.zeros_like(qt)
    q_sub = (jnp.where(row < DIFF_SUB_DIM, qt, zero), jnp.where(row >= DIFF_SUB_DIM, qt, zero))
    stats = ((m1, a1), (m2, a2))
    for m_ref, a_ref in stats:
        m_ref[...] = jnp.full(m_ref.shape, NEG_INF, F32)
        a_ref[...] = jnp.zeros(a_ref.shape, F32)

    def scores(kt, s_ref):
        k = k_ref[pl.ds(pl.multiple_of(kt * t, t), t), :]
        for i in range(2):
            s_ref[i] = _dot(k, q_sub[i])

    def absorb(kt, s_ref, causal):
        vt = _with_sum_rows(vt_ref[kt])
        for i in range(2):
            s = s_ref[i]
            if causal:
                kpos = lax.broadcasted_iota(jnp.int32, s.shape, 0)
                qpos = lax.broadcasted_iota(jnp.int32, s.shape, 1)
                s = jnp.where(kpos <= qpos, s, NEG_INF)
            _online_update(s, vt, *stats[i])

    scores(0, sa_ref)

    def body(j, carry):
        scores(2 * j + 1, sb_ref)
        absorb(2 * j, sa_ref, False)
        scores(2 * j + 2, sa_ref)
        absorb(2 * j + 1, sb_ref, False)
        return carry

    lax.fori_loop(0, qi // 2, body, 0)

    @pl.when(qi % 2 == 0)
    def _():
        absorb(qi, sa_ref, True)

    @pl.when(qi % 2 == 1)
    def _():
        scores(qi, sb_ref)
        absorb(qi - 1, sa_ref, False)
        absorb(qi, sb_ref, True)

    lp = lam_ref[...]
    lam = (jnp.exp(jnp.sum(lp[0:1] * lp[1:2], axis=-1, keepdims=True))
           - jnp.exp(jnp.sum(lp[2:3] * lp[3:4], axis=-1, keepdims=True)) + lambda_init)
    o = (_normalized(a1[...], DIFF_V_DIM) - lam * _normalized(a2[...], DIFF_V_DIM)).T
    o_ref[...] = (_rms(o) * g_ref[...] * (1.0 - lambda_init)).astype(BF16)


def _diff_attn(lam_p, qdt, kd, vdt, subln, lambda_init, t):
    b, s, _ = kd.shape
    nq = s // t
    kernel = functools.partial(_diff_attn_kernel, lambda_init=lambda_init, t=t)
    dv = DIFF_V_DIM + SUM_ROWS
    stat = pltpu.VMEM((1, t), F32)
    acc = pltpu.VMEM((dv, t), F32)
    score_buf = pltpu.VMEM((2, t, t), F32)
    return pl.pallas_call(
        kernel,
        out_shape=jax.ShapeDtypeStruct((b, s, DIFF_W), BF16),
        grid=(b, DIFF_HEADS, nq),
        in_specs=[
            pl.BlockSpec((4, DIFF_SUB_DIM), lambda i, h, j: (0, 0)),
            pl.BlockSpec((None, LANES, t), lambda i, h, j: (i * nq + j, h, 0)),
            pl.BlockSpec((None, s, LANES), lambda i, h, j: (i, 0, h)),
            pl.BlockSpec((nq, LANES, t), lambda i, h, j: (i, h, 0)),
            pl.BlockSpec((1, LANES), lambda i, h, j: (0, 0)),
        ],
        out_specs=pl.BlockSpec((None, t, LANES), lambda i, h, j: (i, j, h)),
        scratch_shapes=[stat, acc, stat, acc, score_buf, score_buf],
        compiler_params=_params(("parallel", "parallel", "arbitrary")),
        name="diff_attn",
    )(lam_p, qdt, kd, vdt, subln)


def _nsa_kernel(qpt_ref, qrt_ref, kc_ref, vct_ref, ks_ref, vst_ref, kw_ref, vwt_ref, ng_ref,
                ovlt_ref, o_ref, m_ref, acc_ref, part_ref, bias_ref, sa_ref, sb_ref,
                *, tq, tk, n_sel, top_n, wlen):
    r_heads = NSA_GROUP_SIZE
    hd = NSA_HEAD_DIM
    g = pl.program_id(1)
    q0 = pl.program_id(2) * tq
    qpos = q0 + lax.broadcasted_iota(jnp.int32, (1, tq), 1)

    def heads(x):
        return jnp.concatenate([x] * r_heads, axis=1)

    def head_cols(x, r):
        return x[:, r * tq:(r + 1) * tq]

    def stack_q(qt_ref):
        qt = qt_ref[...]
        q64 = jnp.concatenate([qt[r * hd:(r + 1) * hd, :] for r in range(r_heads)], axis=1)
        q128 = jnp.concatenate([q64, q64], axis=0)
        half = lax.broadcasted_iota(jnp.int32, q128.shape, 0) // hd
        return jnp.where(half == g, q128, jnp.zeros_like(q128))

    q_plain = stack_q(qpt_ref)
    q_rot = stack_q(qrt_ref)

    def scores(kt, s_ref):
        s_ref[...] = _dot(ks_ref[pl.ds(pl.multiple_of(kt * tk, tk), tk), :], q_rot)

    ncp = kc_ref.shape[0]
    start = pl.multiple_of(jnp.maximum(q0 + tq - wlen, 0), LANES)
    raw_c = _dot(kc_ref[...], q_plain)
    raw_w = _dot(kw_ref[pl.ds(start, wlen), :], q_rot)
    scores(0, sa_ref)

    n_idx = lax.broadcasted_iota(jnp.int32, (ncp, tq), 0)
    c_ok = (n_idx * CMP_STRIDE + (CMP_BLOCK - 1)) <= qpos
    s_c = raw_c + heads(jnp.where(c_ok, 0.0, NEG_INF))
    e_c = jnp.exp2(s_c - jnp.max(s_c, axis=0, keepdims=True))
    p_c = e_c / jnp.sum(e_c, axis=0, keepdims=True) * heads(jnp.where(c_ok, 1.0, 0.0))
    o_c = _dot(vct_ref[0:hd, :], p_c.astype(BF16))

    p_sum = head_cols(p_c, 0)
    for r in range(1, r_heads):
        p_sum = p_sum + head_cols(p_c, r)
    ovlt = ovlt_ref[...]
    nb = ovlt.shape[0]
    imp, rem = None, p_sum
    for term in range(3):
        piece = rem.astype(BF16)
        d = _dot(ovlt, piece)
        imp = d if imp is None else imp + d
        rem = rem - piece.astype(F32)
    blk = lax.broadcasted_iota(jnp.int32, (nb, tq), 0)
    q_blk = qpos // SLC_BLOCK
    valid = blk <= q_blk
    forced = (blk == 0) | (blk == q_blk) | (blk == q_blk - 1)
    score = jnp.where(valid, imp + SLC_FORCED_BONUS * jnp.where(forced, 1.0, 0.0), -1.0)
    rank = jnp.zeros((nb, tq), F32)
    for jp in range(n_sel):
        other = score[jp:jp + 1, :]
        gt = jnp.where(other > score, 1.0, 0.0)
        ge = jnp.where(other >= score, 1.0, 0.0)
        rank = rank + jnp.where(blk > jp, ge, gt)
    bias_ref[...] = jnp.where(valid & (rank < float(top_n)), 0.0, NEG_INF)

    vwt = _with_sum_rows(jnp.concatenate([vwt_ref[start // LANES + j] for j in range(wlen // LANES)], axis=1))
    dist = qpos - (start + lax.broadcasted_iota(jnp.int32, (wlen, tq), 0))
    w_ok = (dist >= 0) & (dist < WINDOW)
    s_w = raw_w + heads(jnp.where(w_ok, 0.0, NEG_INF))
    e_w = jnp.exp2(s_w - jnp.max(s_w, axis=0, keepdims=True))
    o_w = _normalized(_dot(vwt, e_w.astype(BF16)), hd)

    gt_all = jax.nn.sigmoid(ng_ref[...]).T

    def gate(r, c):
        return gt_all[3 * r + c:3 * r + c + 1, :]

    for r in range(r_heads):
        part_ref[r * hd:(r + 1) * hd, :] = gate(r, 0) * head_cols(o_c, r) + gate(r, 2) * head_cols(o_w, r)

    m_ref[...] = jnp.full(m_ref.shape, NEG_INF, F32)
    acc_ref[...] = jnp.zeros(acc_ref.shape, F32)
    sub = tk // LANES
    blk_per_tile = tk // SLC_BLOCK

    def absorb(kt, s_ref, causal):
        vt = _with_sum_rows(jnp.concatenate([vst_ref[kt * sub + j] for j in range(sub)], axis=1))
        bias = jnp.concatenate(
            [jnp.broadcast_to(bias_ref[pl.ds(kt * blk_per_tile + i, 1), :], (SLC_BLOCK, tq))
             for i in range(blk_per_tile)], axis=0)
        if causal:
            kpos = kt * tk + lax.broadcasted_iota(jnp.int32, (tk, tq), 0)
            bias = jnp.where(kpos <= qpos, bias, NEG_INF)
        _online_update(s_ref[...] + heads(bias), vt, m_ref, acc_ref)

    kt_last = q0 // tk

    def sel_body(j, carry):
        scores(2 * j + 1, sb_ref)
        absorb(2 * j, sa_ref, False)
        scores(2 * j + 2, sa_ref)
        absorb(2 * j + 1, sb_ref, False)
        return carry

    lax.fori_loop(0, kt_last // 2, sel_body, 0)

    @pl.when(kt_last % 2 == 0)
    def _():
        absorb(kt_last, sa_ref, True)

    @pl.when(kt_last % 2 == 1)
    def _():
        scores(kt_last, sb_ref)
        absorb(kt_last - 1, sa_ref, False)
        absorb(kt_last, sb_ref, True)

    o_s = _normalized(acc_ref[...], hd)
    outs = [part_ref[r * hd:(r + 1) * hd, :] + gate(r, 1) * head_cols(o_s, r) for r in range(r_heads)]
    o_ref[...] = jnp.concatenate(outs, axis=0).T.astype(BF16)


def _nsa_attn(nqpt, nqrt, cmp_kv, cmp_kvt, ks, vst, kw, vwt, ng, ovlt, b, s, tk):
    tq = LANES
    n_sel = s // SLC_BLOCK
    top_n = min(SLC_TOP_N, n_sel)
    wlen = min(WINDOW + tq, s)
    ncp = cmp_kv.shape[3]
    gw = NSA_GROUP_SIZE * NSA_HEAD_DIM
    nq = s // tq
    kernel = functools.partial(_nsa_kernel, tq=tq, tk=tk, n_sel=n_sel, top_n=top_n, wlen=wlen)
    qt_spec = pl.BlockSpec((None, gw, tq), lambda i, g, j: (i * nq + j, g, 0))
    k_spec = pl.BlockSpec((s, LANES), lambda i, g, j: (i, 0))
    dv = NSA_HEAD_DIM + SUM_ROWS
    vt_spec = pl.BlockSpec((None, s // LANES, NSA_HEAD_DIM, LANES), lambda i, g, j: (g, i, 0, 0))
    cols = NSA_GROUP_SIZE * tq
    return pl.pallas_call(
        kernel,
        out_shape=jax.ShapeDtypeStruct((b, s, NSA_W), BF16),
        grid=(b, NSA_KV_GROUPS, nq),
        in_specs=[
            qt_spec, qt_spec,
            pl.BlockSpec((None, None, None, ncp, LANES), lambda i, g, j: (0, i, g, 0, 0)),
            pl.BlockSpec((None, None, None, LANES, ncp), lambda i, g, j: (1, i, g, 0, 0)),
            k_spec, vt_spec, k_spec, vt_spec,
            pl.BlockSpec((tq, LANES), lambda i, g, j: (i * nq + j, g)),
            pl.BlockSpec(ovlt.shape, lambda i, g, j: (0, 0)),
        ],
        out_specs=pl.BlockSpec((None, tq, gw), lambda i, g, j: (i, j, g)),
        scratch_shapes=[pltpu.VMEM((1, cols), F32),
                        pltpu.VMEM((dv, cols), F32), pltpu.VMEM((gw, tq), F32),
                        pltpu.VMEM((n_sel, tq), F32),
                        pltpu.VMEM((tk, cols), F32), pltpu.VMEM((tk, cols), F32)],
        compiler_params=_params(("parallel", "parallel", "arbitrary")),
        name="nsa_attn",
    )(nqpt, nqrt, cmp_kv, cmp_kvt, ks, vst, kw, vwt, ng, ovlt)


def _merge_out_kernel(x_ref, ya_ref, yb_ref, ga_ref, gb_ref, wa_ref, wb_ref, wo_ref, o_ref, mg_ref,
                      *, tn):
    d = x_ref.shape[1]
    ya = ya_ref[...]
    yb = yb_ref[...]
    for c in range(0, d, tn):
        ta = _dot(ya, wa_ref[:, c:c + tn])
        tb = _dot(yb, wb_ref[:, c:c + tn])
        mg_ref[:, c:c + tn] = (jax.nn.sigmoid(ga_ref[:, c:c + tn]) * ta
                               + jax.nn.sigmoid(gb_ref[:, c:c + tn]) * tb).astype(BF16)
    mg = mg_ref[...]
    for c in range(0, d, tn):
        o_ref[:, c:c + tn] = x_ref[:, c:c + tn] + _dot(mg, wo_ref[:, c:c + tn])


def _merge_out(x2d, ya, yb, zg, wa, wb, wo, tm, tn):
    n, d = x2d.shape
    row = lambda i: (i, 0)
    const = lambda i: (0, 0)
    resident = pl.Buffered(1)
    return pl.pallas_call(
        functools.partial(_merge_out_kernel, tn=tn),
        out_shape=jax.ShapeDtypeStruct((n, d), F32),
        grid=(n // tm,),
        in_specs=[
            pl.BlockSpec((tm, d), row),
            pl.BlockSpec((tm, DIFF_W), row),
            pl.BlockSpec((tm, NSA_W), row),
            pl.BlockSpec((tm, d), lambda i: (i, 0)),
            pl.BlockSpec((tm, d), lambda i: (i, 1)),
            pl.BlockSpec((DIFF_W, d), const, pipeline_mode=resident),
            pl.BlockSpec((NSA_W, d), const, pipeline_mode=resident),
            pl.BlockSpec((d, d), const, pipeline_mode=resident),
        ],
        out_specs=pl.BlockSpec((tm, d), row),
        scratch_shapes=[pltpu.VMEM((tm, d), BF16)],
        compiler_params=_params(("parallel",)),
        name="merge_out",
    )(x2d, ya, yb, zg, zg, wa, wb, wo)


def _mlp_kernel(x_ref, g_ref, wu_ref, wd_ref, o_ref, h_ref, acc_ref):
    f = pl.program_id(1)

    @pl.when(f == 0)
    def _():
        h_ref[...] = (_rms(x_ref[...]) * g_ref[...]).astype(BF16)
        acc_ref[...] = jnp.zeros(acc_ref.shape, F32)

    u = jnp.maximum(_dot(h_ref[...], wu_ref[...]), 0.0)
    acc_ref[...] += _dot((u * u).astype(BF16), wd_ref[...])

    @pl.when(f == pl.num_programs(1) - 1)
    def _():
        o_ref[...] = x_ref[...] + acc_ref[...]


def _mlp(x2d, g, wu, wd, tm, tf):
    n, d = x2d.shape
    ff = wu.shape[1]
    return pl.pallas_call(
        _mlp_kernel,
        out_shape=jax.ShapeDtypeStruct((n, d), F32),
        grid=(n // tm, ff // tf),
        in_specs=[
            pl.BlockSpec((tm, d), lambda i, f: (i, 0)),
            pl.BlockSpec((1, d), lambda i, f: (0, 0)),
            pl.BlockSpec((d, tf), lambda i, f: (0, f)),
            pl.BlockSpec((tf, d), lambda i, f: (f, 0)),
        ],
        out_specs=pl.BlockSpec((tm, d), lambda i, f: (i, 0)),
        scratch_shapes=[pltpu.VMEM((tm, d), BF16), pltpu.VMEM((tm, d), F32)],
        compiler_params=_params(("parallel", "arbitrary")),
        name="mlp",
    )(x2d, g, wu, wd)


def _ple_kernel(x_ref, p_ref, g_ref, wp_ref, wg_ref, o_ref):
    x = x_ref[...]
    e = _rms(_dot(p_ref[...].astype(BF16), wp_ref[...])) * g_ref[...]
    gate = jax.nn.sigmoid(_dot(_rms(x).astype(BF16), wg_ref[...]))
    o_ref[...] = x + gate * e


def _ple(x2d, p2d, g, wp, wg, tm):
    n, d = x2d.shape
    row = lambda i: (i, 0)
    const = lambda i: (0, 0)
    resident = pl.Buffered(1)
    return pl.pallas_call(
        _ple_kernel,
        out_shape=jax.ShapeDtypeStruct((n, d), F32),
        grid=(n // tm,),
        in_specs=[
            pl.BlockSpec((tm, d), row),
            pl.BlockSpec((tm, PLE_DIM), row),
            pl.BlockSpec((1, d), const),
            pl.BlockSpec((PLE_DIM, d), const, pipeline_mode=resident),
            pl.BlockSpec((d, d), const, pipeline_mode=resident),
        ],
        out_specs=pl.BlockSpec((tm, d), row),
        compiler_params=_params(("parallel",)),
        name="ple",
    )(x2d, p2d, g, wp, wg)


def _rope_tables(positions):
    rot = 2 * ROT_HALF
    inv_freq = jnp.power(ROPE_THETA, -jnp.arange(0, rot, 2, dtype=F32) / rot)
    ang = positions.astype(F32)[..., None] * inv_freq
    cos, sin = jnp.cos(ang), jnp.sin(ang)
    n = cos.shape[0] * cos.shape[1]
    cos, sin = cos.reshape(n, ROT_HALF), sin.reshape(n, ROT_HALF)
    rest = NSA_HEAD_DIM - rot
    ones, zeros_r, zeros_h = jnp.ones((n, rest), F32), jnp.zeros((n, rest), F32), jnp.zeros((n, ROT_HALF), F32)
    ra = jnp.concatenate([cos, cos, ones], axis=1)
    rb = jnp.concatenate([-sin, zeros_h, zeros_r], axis=1)
    rc = jnp.concatenate([zeros_h, sin, zeros_r], axis=1)
    return tuple(jnp.tile(t, (1, LANES // NSA_HEAD_DIM)) for t in (ra, rb, rc))


def _regroup_w_in(w_in):
    d = w_in.shape[0]
    ng0 = 3 * DIFF_W + NSA_W + 6 * NSA_KV_W
    per_g = 3 * NSA_GROUP_SIZE
    w_diff = w_in[:, :3 * DIFF_W].astype(BF16)
    parts = [w_in[:, 3 * DIFF_W:ng0]]
    for g in range(NSA_KV_GROUPS):
        parts += [w_in[:, ng0 + g * per_g:ng0 + (g + 1) * per_g], jnp.zeros((d, LANES - per_g), w_in.dtype)]
    w_nsa = jnp.concatenate(parts, axis=1).astype(BF16)
    w_gate = w_in[:, ng0 + 3 * NSA_HEADS:].astype(BF16)
    return w_diff, w_nsa, w_gate


def _pick(n, pref):
    t = min(pref, n)
    while n % t:
        t //= 2
    return t


def kernel(x, p, positions, norm_mix, w_in, diff_q_norm, diff_k_norm, diff_lambda, diff_subln,
           nsa_q_norm, nsa_k_norm, cmp_pos, cmp_w1, cmp_w2, w_proj_diff, w_proj_nsa, w_out,
           norm_mlp, w_mlp_up, w_mlp_down, w_ple_proj, norm_ple, w_ple_gate):
    b, s, d = x.shape
    n = b * s
    assert d == D_MODEL and s % 512 == 0
    ra, rb, rc = _rope_tables(positions)
    seg = (jnp.arange(LANES)[:, None] // NSA_HEAD_DIM == jnp.arange(LANES)[None, :] // NSA_HEAD_DIM).astype(BF16)
    ncp = s // CMP_STRIDE
    n_sel = s // SLC_BLOCK
    assert n_sel % 16 == 0 and n_sel <= LANES
    cs = jnp.arange(ncp)[None, :] * CMP_STRIDE
    ss = jnp.arange(n_sel)[:, None] * SLC_BLOCK
    ovlt = ((cs < ss + SLC_BLOCK) & (cs + CMP_BLOCK > ss)).astype(BF16)
    tk_nsa, t_diff = 512, 512

    x2d = x.reshape(n, d)
    for i in range(p.shape[0]):
        lambda_init = 0.8 - 0.6 * math.exp(-0.3 * i)
        dup = lambda v: jnp.tile(v, LANES // NSA_HEAD_DIM)
        gains = jnp.zeros((8, LANES), F32).at[0].set(dup(diff_q_norm[i])).at[1].set(dup(diff_k_norm[i]))
        gains = gains.at[2].set(dup(nsa_q_norm[i])).at[3].set(dup(nsa_k_norm[i]))

        w_diff, w_nsa, w_gate = _regroup_w_in(w_in[i])
        tm_proj = _pick(n, 1024)
        g_mix = norm_mix[i][None]
        qdt, kd, vdt = _proj_diff(x2d, g_mix, w_diff, ra, rb, rc, gains, seg, tm_proj, 512, t_diff)
        nqpt, nqrt, ks, vst, kw, vwt, xc, ng = _proj_nsa(x2d, g_mix, w_nsa, ra, rb, rc, gains, seg, tm_proj, 512)
        zg = _in_proj(x2d, g_mix, w_gate, tm_proj, 1024)

        half = CMP_STRIDE * NSA_HEAD_DIM
        cmp_kv, cmp_kvt = _compress(
            xc,
            cmp_pos[i].reshape(2, 2, half),
            cmp_w1[i].reshape(2, 2, half, CMP_HIDDEN).astype(BF16),
            jnp.tile(cmp_w2[i], (1, 1, LANES // NSA_HEAD_DIM)).astype(BF16),
            gains[3:4], b)

        ya = _diff_attn(diff_lambda[i], qdt, kd.reshape(b, s, DIFF_W), vdt, diff_subln[i][None],
                        lambda_init, t_diff)
        yb = _nsa_attn(nqpt, nqrt, cmp_kv, cmp_kvt, ks, vst, kw, vwt, ng, ovlt, b, s, tk_nsa)

        x2d = _merge_out(x2d, ya.reshape(n, DIFF_W), yb.reshape(n, NSA_W), zg,
                         w_proj_diff[i].astype(BF16), w_proj_nsa[i].astype(BF16), w_out[i].astype(BF16),
                         _pick(n, 256), 512)
        x2d = _mlp(x2d, norm_mlp[i][None], w_mlp_up[i].astype(BF16), w_mlp_down[i].astype(BF16),
                   _pick(n, 512), 1024)
        x2d = _ple(x2d, p[i].reshape(n, PLE_DIM), norm_ple[i][None], w_ple_proj[i].astype(BF16),
                   w_ple_gate[i].astype(BF16), _pick(n, 512))
    return x2d.reshape(b, s, d)
```

```python
import functools
import math

import jax
import jax.numpy as jnp
from jax import lax
from jax.experimental import pallas as pl
from jax.experimental.pallas import tpu as pltpu

F32 = jnp.float32
BF16 = jnp.bfloat16

D_MODEL = 2048
PLE_DIM = 256
ROPE_THETA = 500000.0
ROPE_FRACTION = 4
NORM_EPS = 1e-6
NEG_INF = -1e30
DIFF_HEADS = 8
DIFF_SUB_DIM = 64
DIFF_V_DIM = 2 * DIFF_SUB_DIM
NSA_HEADS = 16
NSA_KV_GROUPS = 2
NSA_GROUP_SIZE = NSA_HEADS // NSA_KV_GROUPS
NSA_HEAD_DIM = 64
CMP_BLOCK = 32
CMP_STRIDE = 16
CMP_HIDDEN = 256
SLC_BLOCK = 64
SLC_TOP_N = 16
SLC_FORCED_BONUS = 1e4
WINDOW = 512
D_FF = 4 * D_MODEL
DIFF_W = DIFF_HEADS * DIFF_V_DIM
NSA_W = NSA_HEADS * NSA_HEAD_DIM
NSA_KV_W = NSA_KV_GROUPS * NSA_HEAD_DIM
ROT_HALF = NSA_HEAD_DIM // ROPE_FRACTION // 2
QK_SCALE = NSA_HEAD_DIM ** -0.5
Q_SCALE = QK_SCALE * math.log2(math.e)
SUM_ROWS = 16

LANES = 128
VMEM_LIMIT = 56 * 1024 * 1024
NSA_Q_TILE = 2 * LANES

Z_DQ, Z_DK, Z_DV, Z_NQ = 0, 1024, 2048, 3072
Z_KC, Z_VC, Z_KS, Z_VS, Z_KW, Z_VW = 4096, 4224, 4352, 4480, 4608, 4736
PREP_W = 4864
Z_NG = PREP_W


def _params(sem, flags=None):
    return pltpu.CompilerParams(dimension_semantics=sem, vmem_limit_bytes=VMEM_LIMIT, flags=flags)


def _dot(a, b):
    return jnp.dot(a, b, preferred_element_type=F32)


def _rms(x, eps=NORM_EPS):
    return x * lax.rsqrt(jnp.mean(x * x, axis=-1, keepdims=True) + eps)


def _split_dot(a, m_bf16, terms):
    acc = None
    rem = a
    for t in range(terms):
        piece = rem.astype(BF16)
        d = _dot(piece, m_bf16)
        acc = d if acc is None else acc + d
        if t + 1 < terms:
            rem = rem - piece.astype(F32)
    return acc


def _in_proj_kernel(x_ref, g_ref, w_ref, o_ref, h_ref):
    @pl.when(pl.program_id(1) == 0)
    def _():
        h_ref[...] = (_rms(x_ref[...]) * g_ref[...]).astype(BF16)

    o_ref[...] = _dot(h_ref[...], w_ref[...])


def _in_proj(x2d, g, w, tm, tn):
    n, d = x2d.shape
    nout = w.shape[1]
    return pl.pallas_call(
        _in_proj_kernel,
        out_shape=jax.ShapeDtypeStruct((n, nout), F32),
        grid=(n // tm, nout // tn),
        in_specs=[
            pl.BlockSpec((tm, d), lambda i, j: (i, 0)),
            pl.BlockSpec((1, d), lambda i, j: (0, 0)),
            pl.BlockSpec((d, tn), lambda i, j: (0, j)),
        ],
        out_specs=pl.BlockSpec((tm, tn), lambda i, j: (i, j)),
        scratch_shapes=[pltpu.VMEM((tm, d), BF16)],
        compiler_params=_params(("parallel", "arbitrary")),
        name="in_proj",
    )(x2d, g, w)


def _prep_kernel(z_ref, ra_ref, rb_ref, rc_ref, gain_ref, seg_ref,
                 qdt_ref, kd_ref, vdt_ref, nqpt_ref, nqrt_ref, ks_ref, vst_ref, kw_ref, vwt_ref, xc_ref,
                 stage_ref):
    ra, rb, rc = ra_ref[...], rb_ref[...], rc_ref[...]
    seg = seg_ref[...]
    tm = ra.shape[0]
    nt = tm // LANES

    def rope(y):
        return (y * ra + pltpu.roll(y, LANES - ROT_HALF, 1) * rb + pltpu.roll(y, ROT_HALF, 1) * rc)

    def head_norm(xb, gain):
        ss = _split_dot(xb * xb, seg, 2)
        return xb * lax.rsqrt(ss * (1.0 / NSA_HEAD_DIM) + NORM_EPS) * gain

    def zblk(col):
        return z_ref[:, col:col + LANES]

    for hb in range(DIFF_W // LANES):
        c = hb * LANES
        qd = rope(head_norm(zblk(Z_DQ + c), gain_ref[0:1, :])) * Q_SCALE
        qdt_ref[c:c + LANES, :] = qd.T.astype(BF16)
        kd_ref[:, c:c + LANES] = rope(head_norm(zblk(Z_DK + c), gain_ref[1:2, :])).astype(BF16)
        vdt_ref[c:c + LANES, :] = zblk(Z_DV + c).T.astype(BF16)
        yq = head_norm(zblk(Z_NQ + c), gain_ref[2:3, :])
        qpt = (yq * Q_SCALE).T.astype(BF16)
        qrt = (rope(yq) * Q_SCALE).T.astype(BF16)
        for j in range(nt):
            nqpt_ref[j, c:c + LANES, :] = qpt[:, j * LANES:(j + 1) * LANES]
            nqrt_ref[j, c:c + LANES, :] = qrt[:, j * LANES:(j + 1) * LANES]

    ks_ref[...] = rope(head_norm(zblk(Z_KS), gain_ref[3:4, :])).astype(BF16)
    kw_ref[...] = rope(head_norm(zblk(Z_KW), gain_ref[3:4, :])).astype(BF16)
    hd = NSA_HEAD_DIM
    for col, v_out in ((Z_VS, vst_ref), (Z_VW, vwt_ref)):
        vt = zblk(col).T.astype(BF16)
        for g in range(NSA_KV_GROUPS):
            for j in range(nt):
                v_out[g, j] = vt[g * hd:(g + 1) * hd, j * LANES:(j + 1) * LANES]

    nchunk = tm // CMP_STRIDE
    lo = lax.broadcasted_iota(jnp.int32, (nchunk, LANES), 1) < hd
    for kv, col in enumerate((Z_KC, Z_VC)):
        stage_ref[...] = zblk(col)
        for t in range(0, CMP_STRIDE, 2):
            a = stage_ref[pl.ds(t, nchunk, stride=CMP_STRIDE), :]
            b = stage_ref[pl.ds(t + 1, nchunk, stride=CMP_STRIDE), :]
            xc_ref[kv, 0, :, t * hd:(t + 2) * hd] = jnp.where(lo, a, pltpu.roll(b, hd, 1))
            xc_ref[kv, 1, :, t * hd:(t + 2) * hd] = jnp.where(lo, pltpu.roll(a, hd, 1), b)


def _prep(z, ra, rb, rc, gains, seg, tm, t_diff):
    n = z.shape[0]
    nt = tm // LANES
    per = t_diff // tm
    row = lambda i: (i, 0)
    tile_t = lambda i: (i // per, 0, i % per)
    diff_t = jax.ShapeDtypeStruct((n // t_diff, DIFF_W, t_diff), BF16)
    nsa_qt = jax.ShapeDtypeStruct((n // LANES, NSA_W, LANES), BF16)
    nsa_vt = jax.ShapeDtypeStruct((NSA_KV_GROUPS, n // LANES, NSA_HEAD_DIM, LANES), BF16)
    diff_t_spec = pl.BlockSpec((None, DIFF_W, tm), tile_t)
    nsa_qt_spec = pl.BlockSpec((nt, NSA_W, LANES), lambda i: (i, 0, 0))
    nsa_vt_spec = pl.BlockSpec((NSA_KV_GROUPS, nt, NSA_HEAD_DIM, LANES), lambda i: (0, i, 0, 0))
    chunk_w = CMP_STRIDE * NSA_HEAD_DIM
    chunks = jax.ShapeDtypeStruct((2, NSA_KV_GROUPS, n // CMP_STRIDE, chunk_w), F32)
    chunks_spec = pl.BlockSpec((2, NSA_KV_GROUPS, tm // CMP_STRIDE, chunk_w), lambda i: (0, 0, i, 0))
    return pl.pallas_call(
        _prep_kernel,
        out_shape=[diff_t, jax.ShapeDtypeStruct((n, DIFF_W), BF16), diff_t, nsa_qt, nsa_qt,
                   jax.ShapeDtypeStruct((n, LANES), BF16), nsa_vt,
                   jax.ShapeDtypeStruct((n, LANES), BF16), nsa_vt, chunks],
        grid=(n // tm,),
        in_specs=[
            pl.BlockSpec((tm, PREP_W), row),
            pl.BlockSpec((tm, LANES), row),
            pl.BlockSpec((tm, LANES), row),
            pl.BlockSpec((tm, LANES), row),
            pl.BlockSpec((8, LANES), lambda i: (0, 0)),
            pl.BlockSpec((LANES, LANES), lambda i: (0, 0)),
        ],
        out_specs=[diff_t_spec, pl.BlockSpec((tm, DIFF_W), row), diff_t_spec, nsa_qt_spec, nsa_qt_spec,
                   pl.BlockSpec((tm, LANES), row), nsa_vt_spec,
                   pl.BlockSpec((tm, LANES), row), nsa_vt_spec, chunks_spec],
        scratch_shapes=[pltpu.VMEM((tm, LANES), F32)],
        compiler_params=_params(("parallel",)),
        name="prep",
    )(z, ra, rb, rc, gains, seg)


def _compress_kernel(x_ref, pos_ref, w1_ref, w2_ref, gain_ref, o_ref, ot_ref):
    x = x_ref[...]
    nc = x.shape[0]
    a = (x + pos_ref[0:1, :]).astype(BF16)
    b = (x + pos_ref[1:2, :]).astype(BF16)
    u = _dot(a, w1_ref[0])
    v = _dot(b, w1_ref[1])
    pre = u + pltpu.roll(v, nc - 1, 0)
    cdf = 0.5 * (1.0 + jnp.tanh(math.sqrt(2.0 / math.pi) * (pre + 0.044715 * (pre * pre * pre))))
    c = _dot((pre * cdf).astype(BF16), w2_ref[...])
    is_key = pl.program_id(0) == 0
    out = jnp.where(is_key, _rms(c) * gain_ref[...], c)
    o_ref[...] = out.astype(BF16)
    ot_ref[...] = out.T.astype(BF16)


def _compress(xc, pos, w1, w2d, gain, b):
    _, g, n_chunks, cw = xc.shape
    nc = n_chunks // b
    return pl.pallas_call(
        _compress_kernel,
        out_shape=[jax.ShapeDtypeStruct((2, b, g, nc, LANES), BF16),
                   jax.ShapeDtypeStruct((2, b, g, LANES, nc), BF16)],
        grid=(2, b, g),
        in_specs=[
            pl.BlockSpec((None, None, nc, cw), lambda t, i, j: (t, j, i, 0)),
            pl.BlockSpec((None, 2, cw), lambda t, i, j: (t, 0, 0)),
            pl.BlockSpec((None, 2, cw, CMP_HIDDEN), lambda t, i, j: (t, 0, 0, 0)),
            pl.BlockSpec((None, CMP_HIDDEN, LANES), lambda t, i, j: (t, 0, 0)),
            pl.BlockSpec((1, LANES), lambda t, i, j: (0, 0)),
        ],
        out_specs=[pl.BlockSpec((None, None, None, nc, LANES), lambda t, i, j: (t, i, j, 0, 0)),
                   pl.BlockSpec((None, None, None, LANES, nc), lambda t, i, j: (t, i, j, 0, 0))],
        compiler_params=_params(("parallel", "parallel", "parallel")),
        name="compress",
    )(xc, pos, w1, w2d, gain)


def _online_update(s, vt, m_ref, acc_ref):
    m_old = m_ref[...]
    m_new = jnp.maximum(m_old, jnp.max(s, axis=0, keepdims=True))
    p = jnp.exp2(s - m_new)
    acc_ref[...] = jnp.exp2(m_old - m_new) * acc_ref[...] + _dot(vt, p.astype(BF16))
    m_ref[...] = m_new


def _with_sum_rows(vt):
    rows = lax.broadcasted_iota(jnp.int32, (SUM_ROWS, vt.shape[1]), 0)
    return jnp.concatenate([vt, jnp.where(rows == 0, 1.0, 0.0).astype(vt.dtype)], axis=0)


def _normalized(acc, dv):
    return acc[0:dv, :] / acc[dv:dv + 1, :]


def _diff_attn_kernel(lam_ref, qt_ref, k_ref, vt_ref, g_ref, o_ref,
                      m1, a1, m2, a2, sa_ref, sb_ref, *, lambda_init, t):
    qi = pl.program_id(2)
    qt = qt_ref[...]
    row = lax.broadcasted_iota(jnp.int32, qt.shape, 0)
    zero = jnp.zeros_like(qt)
    q_sub = (jnp.where(row < DIFF_SUB_DIM, qt, zero), jnp.where(row >= DIFF_SUB_DIM, qt, zero))
    stats = ((m1, a1), (m2, a2))
    for m_ref, a_ref in stats:
        m_ref[...] = jnp.full(m_ref.shape, NEG_INF, F32)
        a_ref[...] = jnp.zeros(a_ref.shape, F32)

    def scores(kt, s_ref):
        k = k_ref[pl.ds(pl.multiple_of(kt * t, t), t), :]
        for i in range(2):
            s_ref[i] = _dot(k, q_sub[i])

    def absorb(kt, s_ref, causal):
        vt = _with_sum_rows(vt_ref[kt])
        for i in range(2):
            s = s_ref[i]
            if causal:
                kpos = lax.broadcasted_iota(jnp.int32, s.shape, 0)
                qpos = lax.broadcasted_iota(jnp.int32, s.shape, 1)
                s = jnp.where(kpos <= qpos, s, NEG_INF)
            _online_update(s, vt, *stats[i])

    scores(0, sa_ref)

    def body(j, carry):
        scores(2 * j + 1, sb_ref)
        absorb(2 * j, sa_ref, False)
        scores(2 * j + 2, sa_ref)
        absorb(2 * j + 1, sb_ref, False)
        return carry

    lax.fori_loop(0, qi // 2, body, 0)

    @pl.when(qi % 2 == 0)
    def _():
        absorb(qi, sa_ref, True)

    @pl.when(qi % 2 == 1)
    def _():
        scores(qi, sb_ref)
        absorb(qi - 1, sa_ref, False)
        absorb(qi, sb_ref, True)

    lp = lam_ref[...]
    lam = (jnp.exp(jnp.sum(lp[0:1] * lp[1:2], axis=-1, keepdims=True))
           - jnp.exp(jnp.sum(lp[2:3] * lp[3:4], axis=-1, keepdims=True)) + lambda_init)
    o = (_normalized(a1[...], DIFF_V_DIM) - lam * _normalized(a2[...], DIFF_V_DIM)).T
    o_ref[...] = (_rms(o) * g_ref[...] * (1.0 - lambda_init)).astype(BF16)


def _diff_attn(lam_p, qdt, kd, vdt, subln, lambda_init, t):
    b, s, _ = kd.shape
    nq = s // t
    kernel = functools.partial(_diff_attn_kernel, lambda_init=lambda_init, t=t)
    dv = DIFF_V_DIM + SUM_ROWS
    stat = pltpu.VMEM((1, t), F32)
    acc = pltpu.VMEM((dv, t), F32)
    score_buf = pltpu.VMEM((2, t, t), F32)
    return pl.pallas_call(
        kernel,
        out_shape=jax.ShapeDtypeStruct((b, s, DIFF_W), BF16),
        grid=(b, DIFF_HEADS, nq),
        in_specs=[
            pl.BlockSpec((4, DIFF_SUB_DIM), lambda i, h, j: (0, 0)),
            pl.BlockSpec((None, LANES, t), lambda i, h, j: (i * nq + j, h, 0)),
            pl.BlockSpec((None, s, LANES), lambda i, h, j: (i, 0, h)),
            pl.BlockSpec((nq, LANES, t), lambda i, h, j: (i, h, 0)),
            pl.BlockSpec((1, LANES), lambda i, h, j: (0, 0)),
        ],
        out_specs=pl.BlockSpec((None, t, LANES), lambda i, h, j: (i, j, h)),
        scratch_shapes=[stat, acc, stat, acc, score_buf, score_buf],
        compiler_params=_params(("parallel", "parallel", "arbitrary")),
        name="diff_attn",
    )(lam_p, qdt, kd, vdt, subln)


def _nsa_kernel(qpt_ref, qrt_ref, kc_ref, vct_ref, ks_ref, vst_ref, kw_ref, vwt_ref, ng_ref,
                ovlt_ref, o_ref, m_ref, acc_ref, part_ref, bias_ref, sa_ref, sb_ref,
                *, tq, tk, n_sel, top_n, wlen):
    r_heads = NSA_GROUP_SIZE
    hd = NSA_HEAD_DIM
    g = pl.program_id(1)
    q0 = pl.program_id(2) * tq
    qpos = q0 + lax.broadcasted_iota(jnp.int32, (1, tq), 1)

    def heads(x):
        return jnp.concatenate([x] * r_heads, axis=1)

    def head_cols(x, r):
        return x[:, r * tq:(r + 1) * tq]

    def stack_q(qt_ref):
        qt = jnp.concatenate([qt_ref[t] for t in range(tq // LANES)], axis=1)
        q64 = jnp.concatenate([qt[r * hd:(r + 1) * hd, :] for r in range(r_heads)], axis=1)
        q128 = jnp.concatenate([q64, q64], axis=0)
        half = lax.broadcasted_iota(jnp.int32, q128.shape, 0) // hd
        return jnp.where(half == g, q128, jnp.zeros_like(q128))

    q_plain = stack_q(qpt_ref)
    q_rot = stack_q(qrt_ref)

    def scores(kt, s_ref):
        s_ref[...] = _dot(ks_ref[pl.ds(pl.multiple_of(kt * tk, tk), tk), :], q_rot)

    ncp = kc_ref.shape[0]
    start = pl.multiple_of(jnp.maximum(q0 + tq - wlen, 0), LANES)
    raw_c = _dot(kc_ref[...], q_plain)
    raw_w = _dot(kw_ref[pl.ds(start, wlen), :], q_rot)
    scores(0, sa_ref)

    n_idx = lax.broadcasted_iota(jnp.int32, (ncp, tq), 0)
    c_ok = (n_idx * CMP_STRIDE + (CMP_BLOCK - 1)) <= qpos
    s_c = raw_c + heads(jnp.where(c_ok, 0.0, NEG_INF))
    e_c = jnp.exp2(s_c - jnp.max(s_c, axis=0, keepdims=True))
    p_c = e_c / jnp.sum(e_c, axis=0, keepdims=True) * heads(jnp.where(c_ok, 1.0, 0.0))
    o_c = _dot(vct_ref[0:hd, :], p_c.astype(BF16))

    p_sum = head_cols(p_c, 0)
    for r in range(1, r_heads):
        p_sum = p_sum + head_cols(p_c, r)
    ovlt = ovlt_ref[...]
    nb = ovlt.shape[0]
    imp, rem = None, p_sum
    for term in range(3):
        piece = rem.astype(BF16)
        d = _dot(ovlt, piece)
        imp = d if imp is None else imp + d
        rem = rem - piece.astype(F32)
    blk = lax.broadcasted_iota(jnp.int32, (nb, tq), 0)
    q_blk = qpos // SLC_BLOCK
    valid = blk <= q_blk
    forced = (blk == 0) | (blk == q_blk) | (blk == q_blk - 1)
    score = jnp.where(valid, imp + SLC_FORCED_BONUS * jnp.where(forced, 1.0, 0.0), -1.0)
    rank = jnp.zeros((nb, tq), F32)
    for jp in range(n_sel):
        other = score[jp:jp + 1, :]
        gt = jnp.where(other > score, 1.0, 0.0)
        ge = jnp.where(other >= score, 1.0, 0.0)
        rank = rank + jnp.where(blk > jp, ge, gt)
    bias_ref[...] = jnp.where(valid & (rank < float(top_n)), 0.0, NEG_INF)

    vwt = _with_sum_rows(jnp.concatenate([vwt_ref[start // LANES + j] for j in range(wlen // LANES)], axis=1))
    dist = qpos - (start + lax.broadcasted_iota(jnp.int32, (wlen, tq), 0))
    w_ok = (dist >= 0) & (dist < WINDOW)
    s_w = raw_w + heads(jnp.where(w_ok, 0.0, NEG_INF))
    e_w = jnp.exp2(s_w - jnp.max(s_w, axis=0, keepdims=True))
    o_w = _normalized(_dot(vwt, e_w.astype(BF16)), hd)

    gt_all = jax.nn.sigmoid(ng_ref[...]).T

    def gate(r, c):
        return gt_all[3 * r + c:3 * r + c + 1, :]

    for r in range(r_heads):
        part_ref[r * hd:(r + 1) * hd, :] = gate(r, 0) * head_cols(o_c, r) + gate(r, 2) * head_cols(o_w, r)

    m_ref[...] = jnp.full(m_ref.shape, NEG_INF, F32)
    acc_ref[...] = jnp.zeros(acc_ref.shape, F32)
    sub = tk // LANES
    blk_per_tile = tk // SLC_BLOCK

    def absorb(kt, s_ref, causal):
        vt = _with_sum_rows(jnp.concatenate([vst_ref[kt * sub + j] for j in range(sub)], axis=1))
        bias = jnp.concatenate(
            [jnp.broadcast_to(bias_ref[pl.ds(kt * blk_per_tile + i, 1), :], (SLC_BLOCK, tq))
             for i in range(blk_per_tile)], axis=0)
        if causal:
            kpos = kt * tk + lax.broadcasted_iota(jnp.int32, (tk, tq), 0)
            bias = jnp.where(kpos <= qpos, bias, NEG_INF)
        _online_update(s_ref[...] + heads(bias), vt, m_ref, acc_ref)

    kt_last = q0 // tk

    def sel_body(j, carry):
        scores(2 * j + 1, sb_ref)
        absorb(2 * j, sa_ref, False)
        scores(2 * j + 2, sa_ref)
        absorb(2 * j + 1, sb_ref, False)
        return carry

    lax.fori_loop(0, kt_last // 2, sel_body, 0)

    @pl.when(kt_last % 2 == 0)
    def _():
        absorb(kt_last, sa_ref, True)

    @pl.when(kt_last % 2 == 1)
    def _():
        scores(kt_last, sb_ref)
        absorb(kt_last - 1, sa_ref, False)
        absorb(kt_last, sb_ref, True)

    o_s = _normalized(acc_ref[...], hd)
    outs = [part_ref[r * hd:(r + 1) * hd, :] + gate(r, 1) * head_cols(o_s, r) for r in range(r_heads)]
    o_ref[...] = jnp.concatenate(outs, axis=0).T.astype(BF16)


def _nsa_attn(nqpt, nqrt, cmp_kv, cmp_kvt, ks, vst, kw, vwt, z, ovlt, b, s, tk):
    tq = NSA_Q_TILE
    assert tk % tq == 0
    n_sel = s // SLC_BLOCK
    top_n = min(SLC_TOP_N, n_sel)
    wlen = min(WINDOW + tq, s)
    ncp = cmp_kv.shape[3]
    gw = NSA_GROUP_SIZE * NSA_HEAD_DIM
    nq = s // tq
    kernel = functools.partial(_nsa_kernel, tq=tq, tk=tk, n_sel=n_sel, top_n=top_n, wlen=wlen)
    qt_spec = pl.BlockSpec((tq // LANES, gw, LANES), lambda i, g, j: (i * nq + j, g, 0))
    k_spec = pl.BlockSpec((s, LANES), lambda i, g, j: (i, 0))
    dv = NSA_HEAD_DIM + SUM_ROWS
    vt_spec = pl.BlockSpec((None, s // LANES, NSA_HEAD_DIM, LANES), lambda i, g, j: (g, i, 0, 0))
    cols = NSA_GROUP_SIZE * tq
    return pl.pallas_call(
        kernel,
        out_shape=jax.ShapeDtypeStruct((b, s, NSA_W), BF16),
        grid=(b, NSA_KV_GROUPS, nq),
        in_specs=[
            qt_spec, qt_spec,
            pl.BlockSpec((None, None, None, ncp, LANES), lambda i, g, j: (0, i, g, 0, 0)),
            pl.BlockSpec((None, None, None, LANES, ncp), lambda i, g, j: (1, i, g, 0, 0)),
            k_spec, vt_spec, k_spec, vt_spec,
            pl.BlockSpec((tq, LANES), lambda i, g, j: (i * nq + j, Z_NG // LANES + g)),
            pl.BlockSpec(ovlt.shape, lambda i, g, j: (0, 0)),
        ],
        out_specs=pl.BlockSpec((None, tq, gw), lambda i, g, j: (i, j, g)),
        scratch_shapes=[pltpu.VMEM((1, cols), F32),
                        pltpu.VMEM((dv, cols), F32), pltpu.VMEM((gw, tq), F32),
                        pltpu.VMEM((n_sel, tq), F32),
                        pltpu.VMEM((tk, cols), F32), pltpu.VMEM((tk, cols), F32)],
        compiler_params=_params(("parallel", "parallel", "arbitrary")),
        name="nsa_attn",
    )(nqpt, nqrt, cmp_kv, cmp_kvt, ks, vst, kw, vwt, z, ovlt)


def _merge_out_kernel(x_ref, ya_ref, yb_ref, ga_ref, gb_ref, wa_ref, wb_ref, wo_ref, o_ref, mg_ref,
                      *, tn):
    d = x_ref.shape[1]
    ya = ya_ref[...]
    yb = yb_ref[...]
    for c in range(0, d, tn):
        ta = _dot(ya, wa_ref[:, c:c + tn])
        tb = _dot(yb, wb_ref[:, c:c + tn])
        mg_ref[:, c:c + tn] = (jax.nn.sigmoid(ga_ref[:, c:c + tn]) * ta
                               + jax.nn.sigmoid(gb_ref[:, c:c + tn]) * tb).astype(BF16)
    mg = mg_ref[...]
    for c in range(0, d, tn):
        o_ref[:, c:c + tn] = x_ref[:, c:c + tn] + _dot(mg, wo_ref[:, c:c + tn])


def _merge_out(x2d, ya, yb, zg, wa, wb, wo, tm, tn):
    n, d = x2d.shape
    row = lambda i: (i, 0)
    const = lambda i: (0, 0)
    resident = pl.Buffered(1)
    return pl.pallas_call(
        functools.partial(_merge_out_kernel, tn=tn),
        out_shape=jax.ShapeDtypeStruct((n, d), F32),
        grid=(n // tm,),
        in_specs=[
            pl.BlockSpec((tm, d), row),
            pl.BlockSpec((tm, DIFF_W), row),
            pl.BlockSpec((tm, NSA_W), row),
            pl.BlockSpec((tm, d), lambda i: (i, 0)),
            pl.BlockSpec((tm, d), lambda i: (i, 1)),
            pl.BlockSpec((DIFF_W, d), const, pipeline_mode=resident),
            pl.BlockSpec((NSA_W, d), const, pipeline_mode=resident),
            pl.BlockSpec((d, d), const, pipeline_mode=resident),
        ],
        out_specs=pl.BlockSpec((tm, d), row),
        scratch_shapes=[pltpu.VMEM((tm, d), BF16)],
        compiler_params=_params(("parallel",)),
        name="merge_out",
    )(x2d, ya, yb, zg, zg, wa, wb, wo)


def _mlp_kernel(x_ref, g_ref, wu_ref, wd_ref, o_ref, h_ref, acc_ref):
    f = pl.program_id(1)

    @pl.when(f == 0)
    def _():
        h_ref[...] = (_rms(x_ref[...]) * g_ref[...]).astype(BF16)
        acc_ref[...] = jnp.zeros(acc_ref.shape, F32)

    u = jnp.maximum(_dot(h_ref[...], wu_ref[...]), 0.0)
    acc_ref[...] += _dot((u * u).astype(BF16), wd_ref[...])

    @pl.when(f == pl.num_programs(1) - 1)
    def _():
        o_ref[...] = x_ref[...] + acc_ref[...]


def _mlp(x2d, g, wu, wd, tm, tf):
    n, d = x2d.shape
    ff = wu.shape[1]
    return pl.pallas_call(
        _mlp_kernel,
        out_shape=jax.ShapeDtypeStruct((n, d), F32),
        grid=(n // tm, ff // tf),
        in_specs=[
            pl.BlockSpec((tm, d), lambda i, f: (i, 0)),
            pl.BlockSpec((1, d), lambda i, f: (0, 0)),
            pl.BlockSpec((d, tf), lambda i, f: (0, f)),
            pl.BlockSpec((tf, d), lambda i, f: (f, 0)),
        ],
        out_specs=pl.BlockSpec((tm, d), lambda i, f: (i, 0)),
        scratch_shapes=[pltpu.VMEM((tm, d), BF16), pltpu.VMEM((tm, d), F32)],
        compiler_params=_params(("parallel", "arbitrary")),
        name="mlp",
    )(x2d, g, wu, wd)


def _ple_kernel(x_ref, p_ref, g_ref, wp_ref, wg_ref, o_ref):
    x = x_ref[...]
    e = _rms(_dot(p_ref[...].astype(BF16), wp_ref[...])) * g_ref[...]
    gate = jax.nn.sigmoid(_dot(_rms(x).astype(BF16), wg_ref[...]))
    o_ref[...] = x + gate * e


def _ple(x2d, p2d, g, wp, wg, tm):
    n, d = x2d.shape
    row = lambda i: (i, 0)
    const = lambda i: (0, 0)
    resident = pl.Buffered(1)
    return pl.pallas_call(
        _ple_kernel,
        out_shape=jax.ShapeDtypeStruct((n, d), F32),
        grid=(n // tm,),
        in_specs=[
            pl.BlockSpec((tm, d), row),
            pl.BlockSpec((tm, PLE_DIM), row),
            pl.BlockSpec((1, d), const),
            pl.BlockSpec((PLE_DIM, d), const, pipeline_mode=resident),
            pl.BlockSpec((d, d), const, pipeline_mode=resident),
        ],
        out_specs=pl.BlockSpec((tm, d), row),
        compiler_params=_params(("parallel",)),
        name="ple",
    )(x2d, p2d, g, wp, wg)


def _rope_tables(positions):
    rot = 2 * ROT_HALF
    inv_freq = jnp.power(ROPE_THETA, -jnp.arange(0, rot, 2, dtype=F32) / rot)
    ang = positions.astype(F32)[..., None] * inv_freq
    cos, sin = jnp.cos(ang), jnp.sin(ang)
    n = cos.shape[0] * cos.shape[1]
    cos, sin = cos.reshape(n, ROT_HALF), sin.reshape(n, ROT_HALF)
    rest = NSA_HEAD_DIM - rot
    ones, zeros_r, zeros_h = jnp.ones((n, rest), F32), jnp.zeros((n, rest), F32), jnp.zeros((n, ROT_HALF), F32)
    ra = jnp.concatenate([cos, cos, ones], axis=1)
    rb = jnp.concatenate([-sin, zeros_h, zeros_r], axis=1)
    rc = jnp.concatenate([zeros_h, sin, zeros_r], axis=1)
    return tuple(jnp.tile(t, (1, LANES // NSA_HEAD_DIM)) for t in (ra, rb, rc))


def _regroup_w_in(w_in):
    d = w_in.shape[0]
    ng0 = 3 * DIFF_W + NSA_W + 6 * NSA_KV_W
    per_g = 3 * NSA_GROUP_SIZE
    parts = [w_in[:, :ng0]]
    for g in range(NSA_KV_GROUPS):
        parts += [w_in[:, ng0 + g * per_g:ng0 + (g + 1) * per_g], jnp.zeros((d, LANES - per_g), w_in.dtype)]
    w_main = jnp.concatenate(parts, axis=1).astype(BF16)
    w_gate = w_in[:, ng0 + 3 * NSA_HEADS:].astype(BF16)
    return w_main, w_gate


def _pick(n, pref):
    t = min(pref, n)
    while n % t:
        t //= 2
    return t


def kernel(x, p, positions, norm_mix, w_in, diff_q_norm, diff_k_norm, diff_lambda, diff_subln,
           nsa_q_norm, nsa_k_norm, cmp_pos, cmp_w1, cmp_w2, w_proj_diff, w_proj_nsa, w_out,
           norm_mlp, w_mlp_up, w_mlp_down, w_ple_proj, norm_ple, w_ple_gate):
    b, s, d = x.shape
    n = b * s
    assert d == D_MODEL and s % 512 == 0
    ra, rb, rc = _rope_tables(positions)
    seg = (jnp.arange(LANES)[:, None] // NSA_HEAD_DIM == jnp.arange(LANES)[None, :] // NSA_HEAD_DIM).astype(BF16)
    ncp = s // CMP_STRIDE
    n_sel = s // SLC_BLOCK
    assert n_sel % 16 == 0 and n_sel <= LANES
    cs = jnp.arange(ncp)[None, :] * CMP_STRIDE
    ss = jnp.arange(n_sel)[:, None] * SLC_BLOCK
    ovlt = ((cs < ss + SLC_BLOCK) & (cs + CMP_BLOCK > ss)).astype(BF16)
    tk_nsa, t_diff = 512, 512

    x2d = x.reshape(n, d)
    for i in range(p.shape[0]):
        lambda_init = 0.8 - 0.6 * math.exp(-0.3 * i)
        dup = lambda v: jnp.tile(v, LANES // NSA_HEAD_DIM)
        gains = jnp.zeros((8, LANES), F32).at[0].set(dup(diff_q_norm[i])).at[1].set(dup(diff_k_norm[i]))
        gains = gains.at[2].set(dup(nsa_q_norm[i])).at[3].set(dup(nsa_k_norm[i]))

        w_main, w_gate = _regroup_w_in(w_in[i])
        z = _in_proj(x2d, norm_mix[i][None], w_main, _pick(n, 1024), 1024)
        zg = _in_proj(x2d, norm_mix[i][None], w_gate, _pick(n, 1024), 1024)
        qdt, kd, vdt, nqpt, nqrt, ks, vst, kw, vwt, xc = _prep(z, ra, rb, rc, gains, seg,
                                                               _pick(t_diff, 256), t_diff)

        half = CMP_STRIDE * NSA_HEAD_DIM
        cmp_kv, cmp_kvt = _compress(
            xc,
            cmp_pos[i].reshape(2, 2, half),
            cmp_w1[i].reshape(2, 2, half, CMP_HIDDEN).astype(BF16),
            jnp.tile(cmp_w2[i], (1, 1, LANES // NSA_HEAD_DIM)).astype(BF16),
            gains[3:4], b)

        ya = _diff_attn(diff_lambda[i], qdt, kd.reshape(b, s, DIFF_W), vdt, diff_subln[i][None],
                        lambda_init, t_diff)
        yb = _nsa_attn(nqpt, nqrt, cmp_kv, cmp_kvt, ks, vst, kw, vwt, z, ovlt, b, s, tk_nsa)

        x2d = _merge_out(x2d, ya.reshape(n, DIFF_W), yb.reshape(n, NSA_W), zg,
                         w_proj_diff[i].astype(BF16), w_proj_nsa[i].astype(BF16), w_out[i].astype(BF16),
                         _pick(n, 256), 512)
        x2d = _mlp(x2d, norm_mlp[i][None], w_mlp_up[i].astype(BF16), w_mlp_down[i].astype(BF16),
                   _pick(n, 512), 1024)
        x2d = _ple(x2d, p[i].reshape(n, PLE_DIM), norm_ple[i][None], w_ple_proj[i].astype(BF16),
                   w_ple_gate[i].astype(BF16), _pick(n, 512))
    return x2d.reshape(b, s, d)
```

```python
import functools
import math

import jax
import jax.numpy as jnp
from jax import lax
from jax.experimental import pallas as pl
from jax.experimental.pallas import tpu as pltpu

F32 = jnp.float32
BF16 = jnp.bfloat16

D_MODEL = 2048
PLE_DIM = 256
ROPE_THETA = 500000.0
ROPE_FRACTION = 4
NORM_EPS = 1e-6
NEG_INF = -1e30
DIFF_HEADS = 8
DIFF_SUB_DIM = 64
DIFF_V_DIM = 2 * DIFF_SUB_DIM
NSA_HEADS = 16
NSA_KV_GROUPS = 2
NSA_GROUP_SIZE = NSA_HEADS // NSA_KV_GROUPS
NSA_HEAD_DIM = 64
CMP_BLOCK = 32
CMP_STRIDE = 16
CMP_HIDDEN = 256
SLC_BLOCK = 64
SLC_TOP_N = 16
SLC_FORCED_BONUS = 1e4
WINDOW = 512
D_FF = 4 * D_MODEL
DIFF_W = DIFF_HEADS * DIFF_V_DIM
NSA_W = NSA_HEADS * NSA_HEAD_DIM
NSA_KV_W = NSA_KV_GROUPS * NSA_HEAD_DIM
ROT_HALF = NSA_HEAD_DIM // ROPE_FRACTION // 2
QK_SCALE = NSA_HEAD_DIM ** -0.5
Q_SCALE = QK_SCALE * math.log2(math.e)
SUM_ROWS = 16

LANES = 128
VMEM_LIMIT = 56 * 1024 * 1024
NSA_Q_TILE = 2 * LANES

Z_DQ, Z_DK, Z_DV, Z_NQ = 0, 1024, 2048, 3072
Z_KC, Z_VC, Z_KS, Z_VS, Z_KW, Z_VW = 4096, 4224, 4352, 4480, 4608, 4736
PREP_W = 4864
Z_NG = PREP_W


def _params(sem, flags=None):
    return pltpu.CompilerParams(dimension_semantics=sem, vmem_limit_bytes=VMEM_LIMIT, flags=flags)


def _dot(a, b):
    return jnp.dot(a, b, preferred_element_type=F32)


def _rms(x, eps=NORM_EPS):
    return x * lax.rsqrt(jnp.mean(x * x, axis=-1, keepdims=True) + eps)


def _split_dot(a, m_bf16, terms):
    acc = None
    rem = a
    for t in range(terms):
        piece = rem.astype(BF16)
        d = _dot(piece, m_bf16)
        acc = d if acc is None else acc + d
        if t + 1 < terms:
            rem = rem - piece.astype(F32)
    return acc


def _in_proj_kernel(x_ref, g_ref, w_ref, o_ref, h_ref):
    @pl.when(pl.program_id(1) == 0)
    def _():
        h_ref[...] = (_rms(x_ref[...]) * g_ref[...]).astype(BF16)

    o_ref[...] = _dot(h_ref[...], w_ref[...])


def _in_proj(x2d, g, w, tm, tn):
    n, d = x2d.shape
    nout = w.shape[1]
    return pl.pallas_call(
        _in_proj_kernel,
        out_shape=jax.ShapeDtypeStruct((n, nout), F32),
        grid=(n // tm, nout // tn),
        in_specs=[
            pl.BlockSpec((tm, d), lambda i, j: (i, 0)),
            pl.BlockSpec((1, d), lambda i, j: (0, 0)),
            pl.BlockSpec((d, tn), lambda i, j: (0, j)),
        ],
        out_specs=pl.BlockSpec((tm, tn), lambda i, j: (i, j)),
        scratch_shapes=[pltpu.VMEM((tm, d), BF16)],
        compiler_params=_params(("parallel", "arbitrary")),
        name="in_proj",
    )(x2d, g, w)


def _prep_kernel(z_ref, ra_ref, rb_ref, rc_ref, gain_ref, seg_ref,
                 qdt_ref, kd_ref, vdt_ref, nqpt_ref, nqrt_ref, ks_ref, vst_ref, kw_ref, vwt_ref, xc_ref,
                 stage_ref):
    ra, rb, rc = ra_ref[...], rb_ref[...], rc_ref[...]
    seg = seg_ref[...]
    tm = ra.shape[0]
    nt = tm // LANES

    def rope(y):
        return (y * ra + pltpu.roll(y, LANES - ROT_HALF, 1) * rb + pltpu.roll(y, ROT_HALF, 1) * rc)

    def head_norm(xb, gain):
        ss = _split_dot(xb * xb, seg, 2)
        return xb * lax.rsqrt(ss * (1.0 / NSA_HEAD_DIM) + NORM_EPS) * gain

    def zblk(col):
        return z_ref[:, col:col + LANES]

    for hb in range(DIFF_W // LANES):
        c = hb * LANES
        qd = rope(head_norm(zblk(Z_DQ + c), gain_ref[0:1, :])) * Q_SCALE
        qdt_ref[c:c + LANES, :] = qd.T.astype(BF16)
        kd_ref[:, c:c + LANES] = rope(head_norm(zblk(Z_DK + c), gain_ref[1:2, :])).astype(BF16)
        vdt_ref[c:c + LANES, :] = zblk(Z_DV + c).T.astype(BF16)
        yq = head_norm(zblk(Z_NQ + c), gain_ref[2:3, :])
        qpt = (yq * Q_SCALE).T.astype(BF16)
        qrt = (rope(yq) * Q_SCALE).T.astype(BF16)
        for j in range(nt):
            nqpt_ref[j, c:c + LANES, :] = qpt[:, j * LANES:(j + 1) * LANES]
            nqrt_ref[j, c:c + LANES, :] = qrt[:, j * LANES:(j + 1) * LANES]

    ks_ref[...] = rope(head_norm(zblk(Z_KS), gain_ref[3:4, :])).astype(BF16)
    kw_ref[...] = rope(head_norm(zblk(Z_KW), gain_ref[3:4, :])).astype(BF16)
    hd = NSA_HEAD_DIM
    for col, v_out in ((Z_VS, vst_ref), (Z_VW, vwt_ref)):
        vt = zblk(col).T.astype(BF16)
        for g in range(NSA_KV_GROUPS):
            for j in range(nt):
                v_out[g, j] = vt[g * hd:(g + 1) * hd, j * LANES:(j + 1) * LANES]

    nchunk = tm // CMP_STRIDE
    lo = lax.broadcasted_iota(jnp.int32, (nchunk, LANES), 1) < hd
    for kv, col in enumerate((Z_KC, Z_VC)):
        stage_ref[...] = zblk(col)
        for t in range(0, CMP_STRIDE, 2):
            a = stage_ref[pl.ds(t, nchunk, stride=CMP_STRIDE), :]
            b = stage_ref[pl.ds(t + 1, nchunk, stride=CMP_STRIDE), :]
            xc_ref[kv, 0, :, t * hd:(t + 2) * hd] = jnp.where(lo, a, pltpu.roll(b, hd, 1))
            xc_ref[kv, 1, :, t * hd:(t + 2) * hd] = jnp.where(lo, pltpu.roll(a, hd, 1), b)


def _prep(z, ra, rb, rc, gains, seg, tm, t_diff):
    n = z.shape[0]
    nt = tm // LANES
    per = t_diff // tm
    row = lambda i: (i, 0)
    tile_t = lambda i: (i // per, 0, i % per)
    diff_t = jax.ShapeDtypeStruct((n // t_diff, DIFF_W, t_diff), BF16)
    nsa_qt = jax.ShapeDtypeStruct((n // LANES, NSA_W, LANES), BF16)
    nsa_vt = jax.ShapeDtypeStruct((NSA_KV_GROUPS, n // LANES, NSA_HEAD_DIM, LANES), BF16)
    diff_t_spec = pl.BlockSpec((None, DIFF_W, tm), tile_t)
    nsa_qt_spec = pl.BlockSpec((nt, NSA_W, LANES), lambda i: (i, 0, 0))
    nsa_vt_spec = pl.BlockSpec((NSA_KV_GROUPS, nt, NSA_HEAD_DIM, LANES), lambda i: (0, i, 0, 0))
    chunk_w = CMP_STRIDE * NSA_HEAD_DIM
    chunks = jax.ShapeDtypeStruct((2, NSA_KV_GROUPS, n // CMP_STRIDE, chunk_w), F32)
    chunks_spec = pl.BlockSpec((2, NSA_KV_GROUPS, tm // CMP_STRIDE, chunk_w), lambda i: (0, 0, i, 0))
    return pl.pallas_call(
        _prep_kernel,
        out_shape=[diff_t, jax.ShapeDtypeStruct((n, DIFF_W), BF16), diff_t, nsa_qt, nsa_qt,
                   jax.ShapeDtypeStruct((n, LANES), BF16), nsa_vt,
                   jax.ShapeDtypeStruct((n, LANES), BF16), nsa_vt, chunks],
        grid=(n // tm,),
        in_specs=[
            pl.BlockSpec((tm, PREP_W), row),
            pl.BlockSpec((tm, LANES), row),
            pl.BlockSpec((tm, LANES), row),
            pl.BlockSpec((tm, LANES), row),
            pl.BlockSpec((8, LANES), lambda i: (0, 0)),
            pl.BlockSpec((LANES, LANES), lambda i: (0, 0)),
        ],
        out_specs=[diff_t_spec, pl.BlockSpec((tm, DIFF_W), row), diff_t_spec, nsa_qt_spec, nsa_qt_spec,
                   pl.BlockSpec((tm, LANES), row), nsa_vt_spec,
                   pl.BlockSpec((tm, LANES), row), nsa_vt_spec, chunks_spec],
        scratch_shapes=[pltpu.VMEM((tm, LANES), F32)],
        compiler_params=_params(("parallel",)),
        name="prep",
    )(z, ra, rb, rc, gains, seg)


def _compress_kernel(x_ref, pos_ref, w1_ref, w2_ref, gain_ref, o_ref, ot_ref):
    x = x_ref[...]
    nc = x.shape[0]
    a = (x + pos_ref[0:1, :]).astype(BF16)
    b = (x + pos_ref[1:2, :]).astype(BF16)
    u = _dot(a, w1_ref[0])
    v = _dot(b, w1_ref[1])
    pre = u + pltpu.roll(v, nc - 1, 0)
    cdf = 0.5 * (1.0 + jnp.tanh(math.sqrt(2.0 / math.pi) * (pre + 0.044715 * (pre * pre * pre))))
    c = _dot((pre * cdf).astype(BF16), w2_ref[...])
    is_key = pl.program_id(0) == 0
    out = jnp.where(is_key, _rms(c) * gain_ref[...], c)
    o_ref[...] = out.astype(BF16)
    ot_ref[...] = out.T.astype(BF16)


def _compress(xc, pos, w1, w2d, gain, b):
    _, g, n_chunks, cw = xc.shape
    nc = n_chunks // b
    return pl.pallas_call(
        _compress_kernel,
        out_shape=[jax.ShapeDtypeStruct((2, b, g, nc, LANES), BF16),
                   jax.ShapeDtypeStruct((2, b, g, LANES, nc), BF16)],
        grid=(2, b, g),
        in_specs=[
            pl.BlockSpec((None, None, nc, cw), lambda t, i, j: (t, j, i, 0)),
            pl.BlockSpec((None, 2, cw), lambda t, i, j: (t, 0, 0)),
            pl.BlockSpec((None, 2, cw, CMP_HIDDEN), lambda t, i, j: (t, 0, 0, 0)),
            pl.BlockSpec((None, CMP_HIDDEN, LANES), lambda t, i, j: (t, 0, 0)),
            pl.BlockSpec((1, LANES), lambda t, i, j: (0, 0)),
        ],
        out_specs=[pl.BlockSpec((None, None, None, nc, LANES), lambda t, i, j: (t, i, j, 0, 0)),
                   pl.BlockSpec((None, None, None, LANES, nc), lambda t, i, j: (t, i, j, 0, 0))],
        compiler_params=_params(("parallel", "parallel", "parallel")),
        name="compress",
    )(xc, pos, w1, w2d, gain)


def _online_update(s, vt, m_ref, acc_ref, tile_max=None):
    m_old = m_ref[...]
    tile_max = jnp.max(s, axis=0, keepdims=True) if tile_max is None else tile_max
    m_new = jnp.maximum(m_old, tile_max)
    p = jnp.exp2(s - m_new)
    acc_ref[...] = jnp.exp2(m_old - m_new) * acc_ref[...] + _dot(vt, p.astype(BF16))
    m_ref[...] = m_new


def _with_sum_rows(vt):
    rows = lax.broadcasted_iota(jnp.int32, (SUM_ROWS, vt.shape[1]), 0)
    return jnp.concatenate([vt, jnp.where(rows == 0, 1.0, 0.0).astype(vt.dtype)], axis=0)


def _normalized(acc, dv):
    return acc[0:dv, :] / acc[dv:dv + 1, :]


def _diff_attn_kernel(lam_ref, qt_ref, k_ref, vt_ref, g_ref, o_ref,
                      m1, a1, m2, a2, sa_ref, sb_ref, mxa_ref, mxb_ref, *, lambda_init, t):
    qi = pl.program_id(2)
    qt = qt_ref[...]
    row = lax.broadcasted_iota(jnp.int32, qt.shape, 0)
    zero = jnp.zeros_like(qt)
    q_sub = (jnp.where(row < DIFF_SUB_DIM, qt, zero), jnp.where(row >= DIFF_SUB_DIM, qt, zero))
    stats = ((m1, a1), (m2, a2))
    for m_ref, a_ref in stats:
        m_ref[...] = jnp.full(m_ref.shape, NEG_INF, F32)
        a_ref[...] = jnp.zeros(a_ref.shape, F32)

    def scores(kt, buf):
        s_ref, mx_ref = buf
        k = k_ref[pl.ds(pl.multiple_of(kt * t, t), t), :]
        for i in range(2):
            s = _dot(k, q_sub[i])
            s_ref[i] = s
            mx_ref[i] = jnp.max(s, axis=0, keepdims=True)

    def absorb(kt, buf, causal):
        s_ref, mx_ref = buf
        vt = _with_sum_rows(vt_ref[kt])
        for i in range(2):
            s = s_ref[i]
            if causal:
                kpos = lax.broadcasted_iota(jnp.int32, s.shape, 0)
                qpos = lax.broadcasted_iota(jnp.int32, s.shape, 1)
                _online_update(jnp.where(kpos <= qpos, s, NEG_INF), vt, *stats[i])
            else:
                _online_update(s, vt, *stats[i], mx_ref[i])

    buf_a, buf_b = (sa_ref, mxa_ref), (sb_ref, mxb_ref)
    scores(0, buf_a)

    def body(j, carry):
        scores(2 * j + 1, buf_b)
        absorb(2 * j, buf_a, False)
        scores(2 * j + 2, buf_a)
        absorb(2 * j + 1, buf_b, False)
        return carry

    lax.fori_loop(0, qi // 2, body, 0)

    @pl.when(qi % 2 == 0)
    def _():
        absorb(qi, buf_a, True)

    @pl.when(qi % 2 == 1)
    def _():
        scores(qi, buf_b)
        absorb(qi - 1, buf_a, False)
        absorb(qi, buf_b, True)

    lp = lam_ref[...]
    lam = (jnp.exp(jnp.sum(lp[0:1] * lp[1:2], axis=-1, keepdims=True))
           - jnp.exp(jnp.sum(lp[2:3] * lp[3:4], axis=-1, keepdims=True)) + lambda_init)
    o = (_normalized(a1[...], DIFF_V_DIM) - lam * _normalized(a2[...], DIFF_V_DIM)).T
    o_ref[...] = (_rms(o) * g_ref[...] * (1.0 - lambda_init)).astype(BF16)


def _diff_attn(lam_p, qdt, kd, vdt, subln, lambda_init, t):
    b, s, _ = kd.shape
    nq = s // t
    kernel = functools.partial(_diff_attn_kernel, lambda_init=lambda_init, t=t)
    dv = DIFF_V_DIM + SUM_ROWS
    stat = pltpu.VMEM((1, t), F32)
    acc = pltpu.VMEM((dv, t), F32)
    score_buf = pltpu.VMEM((2, t, t), F32)
    return pl.pallas_call(
        kernel,
        out_shape=jax.ShapeDtypeStruct((b, s, DIFF_W), BF16),
        grid=(b, DIFF_HEADS, nq),
        in_specs=[
            pl.BlockSpec((4, DIFF_SUB_DIM), lambda i, h, j: (0, 0)),
            pl.BlockSpec((None, LANES, t), lambda i, h, j: (i * nq + j, h, 0)),
            pl.BlockSpec((None, s, LANES), lambda i, h, j: (i, 0, h)),
            pl.BlockSpec((nq, LANES, t), lambda i, h, j: (i, h, 0)),
            pl.BlockSpec((1, LANES), lambda i, h, j: (0, 0)),
        ],
        out_specs=pl.BlockSpec((None, t, LANES), lambda i, h, j: (i, j, h)),
        scratch_shapes=[stat, acc, stat, acc, score_buf, score_buf,
                        pltpu.VMEM((2, 1, t), F32), pltpu.VMEM((2, 1, t), F32)],
        compiler_params=_params(("parallel", "parallel", "arbitrary")),
        name="diff_attn",
    )(lam_p, qdt, kd, vdt, subln)


def _nsa_kernel(qpt_ref, qrt_ref, kc_ref, vct_ref, ks_ref, vst_ref, kw_ref, vwt_ref, ng_ref,
                ovlt_ref, o_ref, m_ref, acc_ref, part_ref, bias_ref, sa_ref, sb_ref, mxa_ref, mxb_ref,
                *, tq, tk, n_sel, top_n, wlen):
    r_heads = NSA_GROUP_SIZE
    hd = NSA_HEAD_DIM
    g = pl.program_id(1)
    q0 = pl.program_id(2) * tq
    qpos = q0 + lax.broadcasted_iota(jnp.int32, (1, tq), 1)

    def heads(x):
        return jnp.concatenate([x] * r_heads, axis=1)

    def head_cols(x, r):
        return x[:, r * tq:(r + 1) * tq]

    def stack_q(qt_ref):
        qt = jnp.concatenate([qt_ref[t] for t in range(tq // LANES)], axis=1)
        q64 = jnp.concatenate([qt[r * hd:(r + 1) * hd, :] for r in range(r_heads)], axis=1)
        q128 = jnp.concatenate([q64, q64], axis=0)
        half = lax.broadcasted_iota(jnp.int32, q128.shape, 0) // hd
        return jnp.where(half == g, q128, jnp.zeros_like(q128))

    q_plain = stack_q(qpt_ref)
    q_rot = stack_q(qrt_ref)

    ncp = kc_ref.shape[0]
    start = pl.multiple_of(jnp.maximum(q0 + tq - wlen, 0), LANES)
    raw_c = _dot(kc_ref[...], q_plain)
    raw_w = _dot(kw_ref[pl.ds(start, wlen), :], q_rot)

    n_idx = lax.broadcasted_iota(jnp.int32, (ncp, tq), 0)
    c_ok = (n_idx * CMP_STRIDE + (CMP_BLOCK - 1)) <= qpos
    s_c = raw_c + heads(jnp.where(c_ok, 0.0, NEG_INF))
    e_c = jnp.exp2(s_c - jnp.max(s_c, axis=0, keepdims=True))
    p_c = e_c / jnp.sum(e_c, axis=0, keepdims=True) * heads(jnp.where(c_ok, 1.0, 0.0))
    o_c = _dot(vct_ref[0:hd, :], p_c.astype(BF16))

    p_sum = head_cols(p_c, 0)
    for r in range(1, r_heads):
        p_sum = p_sum + head_cols(p_c, r)
    ovlt = ovlt_ref[...]
    nb = ovlt.shape[0]
    imp, rem = None, p_sum
    for term in range(3):
        piece = rem.astype(BF16)
        d = _dot(ovlt, piece)
        imp = d if imp is None else imp + d
        rem = rem - piece.astype(F32)
    blk = lax.broadcasted_iota(jnp.int32, (nb, tq), 0)
    q_blk = qpos // SLC_BLOCK
    valid = blk <= q_blk
    forced = (blk == 0) | (blk == q_blk) | (blk == q_blk - 1)
    score = jnp.where(valid, imp + SLC_FORCED_BONUS * jnp.where(forced, 1.0, 0.0), -1.0)
    rank = jnp.zeros((nb, tq), F32)
    for jp in range(n_sel):
        other = score[jp:jp + 1, :]
        gt = jnp.where(other > score, 1.0, 0.0)
        ge = jnp.where(other >= score, 1.0, 0.0)
        rank = rank + jnp.where(blk > jp, ge, gt)
    bias_ref[...] = jnp.where(valid & (rank < float(top_n)), 0.0, NEG_INF)

    vwt = _with_sum_rows(jnp.concatenate([vwt_ref[start // LANES + j] for j in range(wlen // LANES)], axis=1))
    dist = qpos - (start + lax.broadcasted_iota(jnp.int32, (wlen, tq), 0))
    w_ok = (dist >= 0) & (dist < WINDOW)
    s_w = raw_w + heads(jnp.where(w_ok, 0.0, NEG_INF))
    e_w = jnp.exp2(s_w - jnp.max(s_w, axis=0, keepdims=True))
    o_w = _normalized(_dot(vwt, e_w.astype(BF16)), hd)

    gt_all = jax.nn.sigmoid(ng_ref[...]).T

    def gate(r, c):
        return gt_all[3 * r + c:3 * r + c + 1, :]

    for r in range(r_heads):
        part_ref[r * hd:(r + 1) * hd, :] = gate(r, 0) * head_cols(o_c, r) + gate(r, 2) * head_cols(o_w, r)

    m_ref[...] = jnp.full(m_ref.shape, NEG_INF, F32)
    acc_ref[...] = jnp.zeros(acc_ref.shape, F32)
    sub = tk // LANES
    blk_per_tile = tk // SLC_BLOCK

    def scores(kt, buf):
        s_ref, mx_ref = buf
        bias = jnp.concatenate(
            [jnp.broadcast_to(bias_ref[pl.ds(kt * blk_per_tile + i, 1), :], (SLC_BLOCK, tq))
             for i in range(blk_per_tile)], axis=0)
        kpos = kt * tk + lax.broadcasted_iota(jnp.int32, (tk, tq), 0)
        bias = jnp.where(kpos <= qpos, bias, NEG_INF)
        s = _dot(ks_ref[pl.ds(pl.multiple_of(kt * tk, tk), tk), :], q_rot) + heads(bias)
        s_ref[...] = s
        mx_ref[...] = jnp.max(s, axis=0, keepdims=True)

    def absorb(kt, buf):
        s_ref, mx_ref = buf
        vt = _with_sum_rows(jnp.concatenate([vst_ref[kt * sub + j] for j in range(sub)], axis=1))
        _online_update(s_ref[...], vt, m_ref, acc_ref, mx_ref[...])

    kt_last = q0 // tk
    buf_a, buf_b = (sa_ref, mxa_ref), (sb_ref, mxb_ref)
    scores(0, buf_a)

    def sel_body(j, carry):
        scores(2 * j + 1, buf_b)
        absorb(2 * j, buf_a)
        scores(2 * j + 2, buf_a)
        absorb(2 * j + 1, buf_b)
        return carry

    lax.fori_loop(0, kt_last // 2, sel_body, 0)

    @pl.when(kt_last % 2 == 0)
    def _():
        absorb(kt_last, buf_a)

    @pl.when(kt_last % 2 == 1)
    def _():
        scores(kt_last, buf_b)
        absorb(kt_last - 1, buf_a)
        absorb(kt_last, buf_b)

    o_s = _normalized(acc_ref[...], hd)
    outs = [part_ref[r * hd:(r + 1) * hd, :] + gate(r, 1) * head_cols(o_s, r) for r in range(r_heads)]
    o_ref[...] = jnp.concatenate(outs, axis=0).T.astype(BF16)


def _nsa_attn(nqpt, nqrt, cmp_kv, cmp_kvt, ks, vst, kw, vwt, z, ovlt, b, s, tk):
    tq = NSA_Q_TILE
    assert tk % tq == 0
    n_sel = s // SLC_BLOCK
    top_n = min(SLC_TOP_N, n_sel)
    wlen = min(WINDOW + tq, s)
    ncp = cmp_kv.shape[3]
    gw = NSA_GROUP_SIZE * NSA_HEAD_DIM
    nq = s // tq
    kernel = functools.partial(_nsa_kernel, tq=tq, tk=tk, n_sel=n_sel, top_n=top_n, wlen=wlen)
    qt_spec = pl.BlockSpec((tq // LANES, gw, LANES), lambda i, g, j: (i * nq + j, g, 0))
    k_spec = pl.BlockSpec((s, LANES), lambda i, g, j: (i, 0))
    dv = NSA_HEAD_DIM + SUM_ROWS
    vt_spec = pl.BlockSpec((None, s // LANES, NSA_HEAD_DIM, LANES), lambda i, g, j: (g, i, 0, 0))
    cols = NSA_GROUP_SIZE * tq
    return pl.pallas_call(
        kernel,
        out_shape=jax.ShapeDtypeStruct((b, s, NSA_W), BF16),
        grid=(b, NSA_KV_GROUPS, nq),
        in_specs=[
            qt_spec, qt_spec,
            pl.BlockSpec((None, None, None, ncp, LANES), lambda i, g, j: (0, i, g, 0, 0)),
            pl.BlockSpec((None, None, None, LANES, ncp), lambda i, g, j: (1, i, g, 0, 0)),
            k_spec, vt_spec, k_spec, vt_spec,
            pl.BlockSpec((tq, LANES), lambda i, g, j: (i * nq + j, Z_NG // LANES + g)),
            pl.BlockSpec(ovlt.shape, lambda i, g, j: (0, 0)),
        ],
        out_specs=pl.BlockSpec((None, tq, gw), lambda i, g, j: (i, j, g)),
        scratch_shapes=[pltpu.VMEM((1, cols), F32),
                        pltpu.VMEM((dv, cols), F32), pltpu.VMEM((gw, tq), F32),
                        pltpu.VMEM((n_sel, tq), F32),
                        pltpu.VMEM((tk, cols), F32), pltpu.VMEM((tk, cols), F32),
                        pltpu.VMEM((1, cols), F32), pltpu.VMEM((1, cols), F32)],
        compiler_params=_params(("parallel", "parallel", "arbitrary")),
        name="nsa_attn",
    )(nqpt, nqrt, cmp_kv, cmp_kvt, ks, vst, kw, vwt, z, ovlt)


def _merge_out_kernel(x_ref, ya_ref, yb_ref, ga_ref, gb_ref, wa_ref, wb_ref, wo_ref, o_ref, mg_ref,
                      *, tn):
    d = x_ref.shape[1]
    ya = ya_ref[...]
    yb = yb_ref[...]
    for c in range(0, d, tn):
        ta = _dot(ya, wa_ref[:, c:c + tn])
        tb = _dot(yb, wb_ref[:, c:c + tn])
        mg_ref[:, c:c + tn] = (jax.nn.sigmoid(ga_ref[:, c:c + tn]) * ta
                               + jax.nn.sigmoid(gb_ref[:, c:c + tn]) * tb).astype(BF16)
    mg = mg_ref[...]
    for c in range(0, d, tn):
        o_ref[:, c:c + tn] = x_ref[:, c:c + tn] + _dot(mg, wo_ref[:, c:c + tn])


def _merge_out(x2d, ya, yb, zg, wa, wb, wo, tm, tn):
    n, d = x2d.shape
    row = lambda i: (i, 0)
    const = lambda i: (0, 0)
    resident = pl.Buffered(1)
    return pl.pallas_call(
        functools.partial(_merge_out_kernel, tn=tn),
        out_shape=jax.ShapeDtypeStruct((n, d), F32),
        grid=(n // tm,),
        in_specs=[
            pl.BlockSpec((tm, d), row),
            pl.BlockSpec((tm, DIFF_W), row),
            pl.BlockSpec((tm, NSA_W), row),
            pl.BlockSpec((tm, d), lambda i: (i, 0)),
            pl.BlockSpec((tm, d), lambda i: (i, 1)),
            pl.BlockSpec((DIFF_W, d), const, pipeline_mode=resident),
            pl.BlockSpec((NSA_W, d), const, pipeline_mode=resident),
            pl.BlockSpec((d, d), const, pipeline_mode=resident),
        ],
        out_specs=pl.BlockSpec((tm, d), row),
        scratch_shapes=[pltpu.VMEM((tm, d), BF16)],
        compiler_params=_params(("parallel",)),
        name="merge_out",
    )(x2d, ya, yb, zg, zg, wa, wb, wo)


def _mlp_kernel(x_ref, g_ref, wu_ref, wd_ref, o_ref, h_ref, acc_ref):
    f = pl.program_id(1)

    @pl.when(f == 0)
    def _():
        h_ref[...] = (_rms(x_ref[...]) * g_ref[...]).astype(BF16)
        acc_ref[...] = jnp.zeros(acc_ref.shape, F32)

    u = jnp.maximum(_dot(h_ref[...], wu_ref[...]), 0.0)
    acc_ref[...] += _dot((u * u).astype(BF16), wd_ref[...])

    @pl.when(f == pl.num_programs(1) - 1)
    def _():
        o_ref[...] = x_ref[...] + acc_ref[...]


def _mlp(x2d, g, wu, wd, tm, tf):
    n, d = x2d.shape
    ff = wu.shape[1]
    return pl.pallas_call(
        _mlp_kernel,
        out_shape=jax.ShapeDtypeStruct((n, d), F32),
        grid=(n // tm, ff // tf),
        in_specs=[
            pl.BlockSpec((tm, d), lambda i, f: (i, 0)),
            pl.BlockSpec((1, d), lambda i, f: (0, 0)),
            pl.BlockSpec((d, tf), lambda i, f: (0, f)),
            pl.BlockSpec((tf, d), lambda i, f: (f, 0)),
        ],
        out_specs=pl.BlockSpec((tm, d), lambda i, f: (i, 0)),
        scratch_shapes=[pltpu.VMEM((tm, d), BF16), pltpu.VMEM((tm, d), F32)],
        compiler_params=_params(("parallel", "arbitrary")),
        name="mlp",
    )(x2d, g, wu, wd)


def _ple_kernel(x_ref, p_ref, g_ref, wp_ref, wg_ref, o_ref):
    x = x_ref[...]
    e = _rms(_dot(p_ref[...].astype(BF16), wp_ref[...])) * g_ref[...]
    gate = jax.nn.sigmoid(_dot(_rms(x).astype(BF16), wg_ref[...]))
    o_ref[...] = x + gate * e


def _ple(x2d, p2d, g, wp, wg, tm):
    n, d = x2d.shape
    row = lambda i: (i, 0)
    const = lambda i: (0, 0)
    resident = pl.Buffered(1)
    return pl.pallas_call(
        _ple_kernel,
        out_shape=jax.ShapeDtypeStruct((n, d), F32),
        grid=(n // tm,),
        in_specs=[
            pl.BlockSpec((tm, d), row),
            pl.BlockSpec((tm, PLE_DIM), row),
            pl.BlockSpec((1, d), const),
            pl.BlockSpec((PLE_DIM, d), const, pipeline_mode=resident),
            pl.BlockSpec((d, d), const, pipeline_mode=resident),
        ],
        out_specs=pl.BlockSpec((tm, d), row),
        compiler_params=_params(("parallel",)),
        name="ple",
    )(x2d, p2d, g, wp, wg)


def _rope_tables(positions):
    rot = 2 * ROT_HALF
    inv_freq = jnp.power(ROPE_THETA, -jnp.arange(0, rot, 2, dtype=F32) / rot)
    ang = positions.astype(F32)[..., None] * inv_freq
    cos, sin = jnp.cos(ang), jnp.sin(ang)
    n = cos.shape[0] * cos.shape[1]
    cos, sin = cos.reshape(n, ROT_HALF), sin.reshape(n, ROT_HALF)
    rest = NSA_HEAD_DIM - rot
    ones, zeros_r, zeros_h = jnp.ones((n, rest), F32), jnp.zeros((n, rest), F32), jnp.zeros((n, ROT_HALF), F32)
    ra = jnp.concatenate([cos, cos, ones], axis=1)
    rb = jnp.concatenate([-sin, zeros_h, zeros_r], axis=1)
    rc = jnp.concatenate([zeros_h, sin, zeros_r], axis=1)
    return tuple(jnp.tile(t, (1, LANES // NSA_HEAD_DIM)) for t in (ra, rb, rc))


def _regroup_w_in(w_in):
    d = w_in.shape[0]
    ng0 = 3 * DIFF_W + NSA_W + 6 * NSA_KV_W
    per_g = 3 * NSA_GROUP_SIZE
    parts = [w_in[:, :ng0]]
    for g in range(NSA_KV_GROUPS):
        parts += [w_in[:, ng0 + g * per_g:ng0 + (g + 1) * per_g], jnp.zeros((d, LANES - per_g), w_in.dtype)]
    w_main = jnp.concatenate(parts, axis=1).astype(BF16)
    w_gate = w_in[:, ng0 + 3 * NSA_HEADS:].astype(BF16)
    return w_main, w_gate


def _pick(n, pref):
    t = min(pref, n)
    while n % t:
        t //= 2
    return t


def kernel(x, p, positions, norm_mix, w_in, diff_q_norm, diff_k_norm, diff_lambda, diff_subln,
           nsa_q_norm, nsa_k_norm, cmp_pos, cmp_w1, cmp_w2, w_proj_diff, w_proj_nsa, w_out,
           norm_mlp, w_mlp_up, w_mlp_down, w_ple_proj, norm_ple, w_ple_gate):
    b, s, d = x.shape
    n = b * s
    assert d == D_MODEL and s % 512 == 0
    ra, rb, rc = _rope_tables(positions)
    seg = (jnp.arange(LANES)[:, None] // NSA_HEAD_DIM == jnp.arange(LANES)[None, :] // NSA_HEAD_DIM).astype(BF16)
    ncp = s // CMP_STRIDE
    n_sel = s // SLC_BLOCK
    assert n_sel % 16 == 0 and n_sel <= LANES
    cs = jnp.arange(ncp)[None, :] * CMP_STRIDE
    ss = jnp.arange(n_sel)[:, None] * SLC_BLOCK
    ovlt = ((cs < ss + SLC_BLOCK) & (cs + CMP_BLOCK > ss)).astype(BF16)
    tk_nsa, t_diff = 512, 512

    x2d = x.reshape(n, d)
    for i in range(p.shape[0]):
        lambda_init = 0.8 - 0.6 * math.exp(-0.3 * i)
        dup = lambda v: jnp.tile(v, LANES // NSA_HEAD_DIM)
        gains = jnp.zeros((8, LANES), F32).at[0].set(dup(diff_q_norm[i])).at[1].set(dup(diff_k_norm[i]))
        gains = gains.at[2].set(dup(nsa_q_norm[i])).at[3].set(dup(nsa_k_norm[i]))

        w_main, w_gate = _regroup_w_in(w_in[i])
        z = _in_proj(x2d, norm_mix[i][None], w_main, _pick(n, 1024), 1024)
        zg = _in_proj(x2d, norm_mix[i][None], w_gate, _pick(n, 1024), 1024)
        qdt, kd, vdt, nqpt, nqrt, ks, vst, kw, vwt, xc = _prep(z, ra, rb, rc, gains, seg,
                                                               _pick(t_diff, 256), t_diff)

        half = CMP_STRIDE * NSA_HEAD_DIM
        cmp_kv, cmp_kvt = _compress(
            xc,
            cmp_pos[i].reshape(2, 2, half),
            cmp_w1[i].reshape(2, 2, half, CMP_HIDDEN).astype(BF16),
            jnp.tile(cmp_w2[i], (1, 1, LANES // NSA_HEAD_DIM)).astype(BF16),
            gains[3:4], b)

        ya = _diff_attn(diff_lambda[i], qdt, kd.reshape(b, s, DIFF_W), vdt, diff_subln[i][None],
                        lambda_init, t_diff)
        yb = _nsa_attn(nqpt, nqrt, cmp_kv, cmp_kvt, ks, vst, kw, vwt, z, ovlt, b, s, tk_nsa)

        x2d = _merge_out(x2d, ya.reshape(n, DIFF_W), yb.reshape(n, NSA_W), zg,
                         w_proj_diff[i].astype(BF16), w_proj_nsa[i].astype(BF16), w_out[i].astype(BF16),
                         _pick(n, 256), 512)
        x2d = _mlp(x2d, norm_mlp[i][None], w_mlp_up[i].astype(BF16), w_mlp_down[i].astype(BF16),
                   _pick(n, 512), 1024)
        x2d = _ple(x2d, p[i].reshape(n, PLE_DIM), norm_ple[i][None], w_ple_proj[i].astype(BF16),
                   w_ple_gate[i].astype(BF16), _pick(n, 512))
    return x2d.reshape(b, s, d)
```

```python
import functools
import math

import jax
import jax.numpy as jnp
from jax import lax
from jax.experimental import pallas as pl
from jax.experimental.pallas import tpu as pltpu

F32 = jnp.float32
BF16 = jnp.bfloat16

D_MODEL = 2048
PLE_DIM = 256
ROPE_THETA = 500000.0
ROPE_FRACTION = 4
NORM_EPS = 1e-6
NEG_INF = -1e30
DIFF_HEADS = 8
DIFF_SUB_DIM = 64
DIFF_V_DIM = 2 * DIFF_SUB_DIM
NSA_HEADS = 16
NSA_KV_GROUPS = 2
NSA_GROUP_SIZE = NSA_HEADS // NSA_KV_GROUPS
NSA_HEAD_DIM = 64
CMP_BLOCK = 32
CMP_STRIDE = 16
CMP_HIDDEN = 256
SLC_BLOCK = 64
SLC_TOP_N = 16
SLC_FORCED_BONUS = 1e4
WINDOW = 512
D_FF = 4 * D_MODEL
DIFF_W = DIFF_HEADS * DIFF_V_DIM
NSA_W = NSA_HEADS * NSA_HEAD_DIM
NSA_KV_W = NSA_KV_GROUPS * NSA_HEAD_DIM
ROT_HALF = NSA_HEAD_DIM // ROPE_FRACTION // 2
QK_SCALE = NSA_HEAD_DIM ** -0.5
Q_SCALE = QK_SCALE * math.log2(math.e)
SUM_ROWS = 16

LANES = 128
VMEM_LIMIT = 56 * 1024 * 1024
NSA_Q_TILE = 2 * LANES

Z_DQ, Z_DK, Z_DV, Z_NQ = 0, 1024, 2048, 3072
Z_KC, Z_VC, Z_KS, Z_VS, Z_KW, Z_VW = 4096, 4224, 4352, 4480, 4608, 4736
PREP_W = 4864
Z_NG = PREP_W


def _params(sem, flags=None):
    return pltpu.CompilerParams(dimension_semantics=sem, vmem_limit_bytes=VMEM_LIMIT, flags=flags)


def _dot(a, b):
    return jnp.dot(a, b, preferred_element_type=F32)


def _rms(x, eps=NORM_EPS):
    return x * lax.rsqrt(jnp.mean(x * x, axis=-1, keepdims=True) + eps)


def _split_dot(a, m_bf16, terms):
    acc = None
    rem = a
    for t in range(terms):
        piece = rem.astype(BF16)
        d = _dot(piece, m_bf16)
        acc = d if acc is None else acc + d
        if t + 1 < terms:
            rem = rem - piece.astype(F32)
    return acc


def _in_proj_kernel(x_ref, g_ref, w_ref, o_ref, h_ref):
    @pl.when(pl.program_id(1) == 0)
    def _():
        h_ref[...] = (_rms(x_ref[...]) * g_ref[...]).astype(BF16)

    o_ref[...] = _dot(h_ref[...], w_ref[...])


def _in_proj(x2d, g, w, tm, tn):
    n, d = x2d.shape
    nout = w.shape[1]
    return pl.pallas_call(
        _in_proj_kernel,
        out_shape=jax.ShapeDtypeStruct((n, nout), F32),
        grid=(n // tm, nout // tn),
        in_specs=[
            pl.BlockSpec((tm, d), lambda i, j: (i, 0)),
            pl.BlockSpec((1, d), lambda i, j: (0, 0)),
            pl.BlockSpec((d, tn), lambda i, j: (0, j)),
        ],
        out_specs=pl.BlockSpec((tm, tn), lambda i, j: (i, j)),
        scratch_shapes=[pltpu.VMEM((tm, d), BF16)],
        compiler_params=_params(("parallel", "arbitrary")),
        name="in_proj",
    )(x2d, g, w)


def _prep_kernel(z_ref, ra_ref, rb_ref, rc_ref, gain_ref, seg_ref,
                 qdt_ref, kd_ref, vdt_ref, nqpt_ref, nqrt_ref, ks_ref, vst_ref, kw_ref, vwt_ref, xc_ref,
                 stage_ref):
    ra, rb, rc = ra_ref[...], rb_ref[...], rc_ref[...]
    seg = seg_ref[...]
    tm = ra.shape[0]
    nt = tm // LANES

    def rope(y):
        return (y * ra + pltpu.roll(y, LANES - ROT_HALF, 1) * rb + pltpu.roll(y, ROT_HALF, 1) * rc)

    def head_norm(xb, gain):
        ss = _split_dot(xb * xb, seg, 2)
        return xb * lax.rsqrt(ss * (1.0 / NSA_HEAD_DIM) + NORM_EPS) * gain

    def zblk(col):
        return z_ref[:, col:col + LANES]

    for hb in range(DIFF_W // LANES):
        c = hb * LANES
        qd = rope(head_norm(zblk(Z_DQ + c), gain_ref[0:1, :])) * Q_SCALE
        qdt_ref[c:c + LANES, :] = qd.T.astype(BF16)
        kd_ref[:, c:c + LANES] = rope(head_norm(zblk(Z_DK + c), gain_ref[1:2, :])).astype(BF16)
        vdt_ref[c:c + LANES, :] = zblk(Z_DV + c).T.astype(BF16)
        yq = head_norm(zblk(Z_NQ + c), gain_ref[2:3, :])
        qpt = (yq * Q_SCALE).T.astype(BF16)
        qrt = (rope(yq) * Q_SCALE).T.astype(BF16)
        for j in range(nt):
            nqpt_ref[j, c:c + LANES, :] = qpt[:, j * LANES:(j + 1) * LANES]
            nqrt_ref[j, c:c + LANES, :] = qrt[:, j * LANES:(j + 1) * LANES]

    ks_ref[...] = rope(head_norm(zblk(Z_KS), gain_ref[3:4, :])).astype(BF16)
    kw_ref[...] = rope(head_norm(zblk(Z_KW), gain_ref[3:4, :])).astype(BF16)
    hd = NSA_HEAD_DIM
    for col, v_out in ((Z_VS, vst_ref), (Z_VW, vwt_ref)):
        vt = zblk(col).T.astype(BF16)
        for g in range(NSA_KV_GROUPS):
            for j in range(nt):
                v_out[g, j] = vt[g * hd:(g + 1) * hd, j * LANES:(j + 1) * LANES]

    nchunk = tm // CMP_STRIDE
    lo = lax.broadcasted_iota(jnp.int32, (nchunk, LANES), 1) < hd
    for kv, col in enumerate((Z_KC, Z_VC)):
        stage_ref[...] = zblk(col)
        for t in range(0, CMP_STRIDE, 2):
            a = stage_ref[pl.ds(t, nchunk, stride=CMP_STRIDE), :]
            b = stage_ref[pl.ds(t + 1, nchunk, stride=CMP_STRIDE), :]
            xc_ref[kv, 0, :, t * hd:(t + 2) * hd] = jnp.where(lo, a, pltpu.roll(b, hd, 1))
            xc_ref[kv, 1, :, t * hd:(t + 2) * hd] = jnp.where(lo, pltpu.roll(a, hd, 1), b)


def _prep(z, ra, rb, rc, gains, seg, tm, t_diff):
    n = z.shape[0]
    nt = tm // LANES
    per = t_diff // tm
    row = lambda i: (i, 0)
    tile_t = lambda i: (i // per, 0, i % per)
    diff_t = jax.ShapeDtypeStruct((n // t_diff, DIFF_W, t_diff), BF16)
    nsa_qt = jax.ShapeDtypeStruct((n // LANES, NSA_W, LANES), BF16)
    nsa_vt = jax.ShapeDtypeStruct((NSA_KV_GROUPS, n // LANES, NSA_HEAD_DIM, LANES), BF16)
    diff_t_spec = pl.BlockSpec((None, DIFF_W, tm), tile_t)
    nsa_qt_spec = pl.BlockSpec((nt, NSA_W, LANES), lambda i: (i, 0, 0))
    nsa_vt_spec = pl.BlockSpec((NSA_KV_GROUPS, nt, NSA_HEAD_DIM, LANES), lambda i: (0, i, 0, 0))
    chunk_w = CMP_STRIDE * NSA_HEAD_DIM
    chunks = jax.ShapeDtypeStruct((2, NSA_KV_GROUPS, n // CMP_STRIDE, chunk_w), F32)
    chunks_spec = pl.BlockSpec((2, NSA_KV_GROUPS, tm // CMP_STRIDE, chunk_w), lambda i: (0, 0, i, 0))
    return pl.pallas_call(
        _prep_kernel,
        out_shape=[diff_t, jax.ShapeDtypeStruct((n, DIFF_W), BF16), diff_t, nsa_qt, nsa_qt,
                   jax.ShapeDtypeStruct((n, LANES), BF16), nsa_vt,
                   jax.ShapeDtypeStruct((n, LANES), BF16), nsa_vt, chunks],
        grid=(n // tm,),
        in_specs=[
            pl.BlockSpec((tm, PREP_W), row),
            pl.BlockSpec((tm, LANES), row),
            pl.BlockSpec((tm, LANES), row),
            pl.BlockSpec((tm, LANES), row),
            pl.BlockSpec((8, LANES), lambda i: (0, 0)),
            pl.BlockSpec((LANES, LANES), lambda i: (0, 0)),
        ],
        out_specs=[diff_t_spec, pl.BlockSpec((tm, DIFF_W), row), diff_t_spec, nsa_qt_spec, nsa_qt_spec,
                   pl.BlockSpec((tm, LANES), row), nsa_vt_spec,
                   pl.BlockSpec((tm, LANES), row), nsa_vt_spec, chunks_spec],
        scratch_shapes=[pltpu.VMEM((tm, LANES), F32)],
        compiler_params=_params(("parallel",)),
        name="prep",
    )(z, ra, rb, rc, gains, seg)


def _compress_kernel(x_ref, pos_ref, w1_ref, w2_ref, gain_ref, o_ref, ot_ref):
    x = x_ref[...]
    nc = x.shape[0]
    a = (x + pos_ref[0:1, :]).astype(BF16)
    b = (x + pos_ref[1:2, :]).astype(BF16)
    u = _dot(a, w1_ref[0])
    v = _dot(b, w1_ref[1])
    pre = u + pltpu.roll(v, nc - 1, 0)
    cdf = 0.5 * (1.0 + jnp.tanh(math.sqrt(2.0 / math.pi) * (pre + 0.044715 * (pre * pre * pre))))
    c = _dot((pre * cdf).astype(BF16), w2_ref[...])
    is_key = pl.program_id(0) == 0
    out = jnp.where(is_key, _rms(c) * gain_ref[...], c)
    o_ref[...] = out.astype(BF16)
    ot_ref[...] = out.T.astype(BF16)


def _compress(xc, pos, w1, w2d, gain, b):
    _, g, n_chunks, cw = xc.shape
    nc = n_chunks // b
    return pl.pallas_call(
        _compress_kernel,
        out_shape=[jax.ShapeDtypeStruct((2, b, g, nc, LANES), BF16),
                   jax.ShapeDtypeStruct((2, b, g, LANES, nc), BF16)],
        grid=(2, b, g),
        in_specs=[
            pl.BlockSpec((None, None, nc, cw), lambda t, i, j: (t, j, i, 0)),
            pl.BlockSpec((None, 2, cw), lambda t, i, j: (t, 0, 0)),
            pl.BlockSpec((None, 2, cw, CMP_HIDDEN), lambda t, i, j: (t, 0, 0, 0)),
            pl.BlockSpec((None, CMP_HIDDEN, LANES), lambda t, i, j: (t, 0, 0)),
            pl.BlockSpec((1, LANES), lambda t, i, j: (0, 0)),
        ],
        out_specs=[pl.BlockSpec((None, None, None, nc, LANES), lambda t, i, j: (t, i, j, 0, 0)),
                   pl.BlockSpec((None, None, None, LANES, nc), lambda t, i, j: (t, i, j, 0, 0))],
        compiler_params=_params(("parallel", "parallel", "parallel")),
        name="compress",
    )(xc, pos, w1, w2d, gain)


def _online_update(s, vt, m_ref, acc_ref, tile_max=None):
    m_old = m_ref[...]
    tile_max = jnp.max(s, axis=0, keepdims=True) if tile_max is None else tile_max
    m_new = jnp.maximum(m_old, tile_max)
    p = jnp.exp2(s - m_new)
    acc_ref[...] = jnp.exp2(m_old - m_new) * acc_ref[...] + _dot(vt, p.astype(BF16))
    m_ref[...] = m_new


def _with_sum_rows(vt):
    rows = lax.broadcasted_iota(jnp.int32, (SUM_ROWS, vt.shape[1]), 0)
    return jnp.concatenate([vt, jnp.where(rows == 0, 1.0, 0.0).astype(vt.dtype)], axis=0)


def _normalized(acc, dv):
    return acc[0:dv, :] / acc[dv:dv + 1, :]


def _diff_attn_kernel(lam_ref, qt_ref, k_ref, vt_ref, g_ref, o_ref,
                      m1, a1, m2, a2, sa_ref, sb_ref, mxa_ref, mxb_ref, *, lambda_init, t):
    qi = pl.program_id(2)
    nq = qt_ref.shape[0]

    def sub_heads(tile):
        qt = qt_ref[tile]
        row = lax.broadcasted_iota(jnp.int32, qt.shape, 0)
        zero = jnp.zeros_like(qt)
        return jnp.where(row < DIFF_SUB_DIM, qt, zero), jnp.where(row >= DIFF_SUB_DIM, qt, zero)

    q_sub = sub_heads(qi)
    stats = ((m1, a1), (m2, a2))
    for m_ref, a_ref in stats:
        m_ref[...] = jnp.full(m_ref.shape, NEG_INF, F32)
        a_ref[...] = jnp.zeros(a_ref.shape, F32)

    def scores(kt, buf, q_pair=q_sub):
        s_ref, mx_ref = buf
        k = k_ref[pl.ds(pl.multiple_of(kt * t, t), t), :]
        for i in range(2):
            s = _dot(k, q_pair[i])
            s_ref[i] = s
            mx_ref[i] = jnp.max(s, axis=0, keepdims=True)

    def absorb(kt, buf, causal):
        s_ref, mx_ref = buf
        vt = _with_sum_rows(vt_ref[kt])
        for i in range(2):
            s = s_ref[i]
            if causal:
                kpos = lax.broadcasted_iota(jnp.int32, s.shape, 0)
                qpos = lax.broadcasted_iota(jnp.int32, s.shape, 1)
                _online_update(jnp.where(kpos <= qpos, s, NEG_INF), vt, *stats[i])
            else:
                _online_update(s, vt, *stats[i], mx_ref[i])

    buf_a, buf_b = (sa_ref, mxa_ref), (sb_ref, mxb_ref)

    @pl.when(qi == 0)
    def _():
        scores(0, buf_a)

    def body(j, carry):
        scores(2 * j + 1, buf_b)
        absorb(2 * j, buf_a, False)
        scores(2 * j + 2, buf_a)
        absorb(2 * j + 1, buf_b, False)
        return carry

    lax.fori_loop(0, qi // 2, body, 0)

    @pl.when(qi % 2 == 0)
    def _():
        absorb(qi, buf_a, True)

    @pl.when(qi % 2 == 1)
    def _():
        scores(qi, buf_b)
        absorb(qi - 1, buf_a, False)
        absorb(qi, buf_b, True)

    lp = lam_ref[...]
    lam = (jnp.exp(jnp.sum(lp[0:1] * lp[1:2], axis=-1, keepdims=True))
           - jnp.exp(jnp.sum(lp[2:3] * lp[3:4], axis=-1, keepdims=True)) + lambda_init)
    o = (_normalized(a1[...], DIFF_V_DIM) - lam * _normalized(a2[...], DIFF_V_DIM)).T
    o_ref[...] = (_rms(o) * g_ref[...] * (1.0 - lambda_init)).astype(BF16)

    scores(0, buf_a, sub_heads(jnp.minimum(qi + 1, nq - 1)))


def _diff_attn(lam_p, qdt, kd, vdt, subln, lambda_init, t):
    b, s, _ = kd.shape
    nq = s // t
    kernel = functools.partial(_diff_attn_kernel, lambda_init=lambda_init, t=t)
    dv = DIFF_V_DIM + SUM_ROWS
    stat = pltpu.VMEM((1, t), F32)
    acc = pltpu.VMEM((dv, t), F32)
    score_buf = pltpu.VMEM((2, t, t), F32)
    return pl.pallas_call(
        kernel,
        out_shape=jax.ShapeDtypeStruct((b, s, DIFF_W), BF16),
        grid=(b, DIFF_HEADS, nq),
        in_specs=[
            pl.BlockSpec((4, DIFF_SUB_DIM), lambda i, h, j: (0, 0)),
            pl.BlockSpec((nq, LANES, t), lambda i, h, j: (i, h, 0)),
            pl.BlockSpec((None, s, LANES), lambda i, h, j: (i, 0, h)),
            pl.BlockSpec((nq, LANES, t), lambda i, h, j: (i, h, 0)),
            pl.BlockSpec((1, LANES), lambda i, h, j: (0, 0)),
        ],
        out_specs=pl.BlockSpec((None, t, LANES), lambda i, h, j: (i, j, h)),
        scratch_shapes=[stat, acc, stat, acc, score_buf, score_buf,
                        pltpu.VMEM((2, 1, t), F32), pltpu.VMEM((2, 1, t), F32)],
        compiler_params=_params(("parallel", "parallel", "arbitrary")),
        name="diff_attn",
    )(lam_p, qdt, kd, vdt, subln)


def _nsa_kernel(qpt_ref, qrt_ref, kc_ref, vct_ref, ks_ref, vst_ref, kw_ref, vwt_ref, ng_ref,
                ovlt_ref, o_ref, m_ref, acc_ref, part_ref, bias_ref, sa_ref, sb_ref, mxa_ref, mxb_ref,
                *, tq, tk, n_sel, top_n, wlen):
    r_heads = NSA_GROUP_SIZE
    hd = NSA_HEAD_DIM
    g = pl.program_id(1)
    q0 = pl.program_id(2) * tq
    qpos = q0 + lax.broadcasted_iota(jnp.int32, (1, tq), 1)

    def heads(x):
        return jnp.concatenate([x] * r_heads, axis=1)

    def head_cols(x, r):
        return x[:, r * tq:(r + 1) * tq]

    def stack_q(qt_ref):
        qt = jnp.concatenate([qt_ref[t] for t in range(tq // LANES)], axis=1)
        q64 = jnp.concatenate([qt[r * hd:(r + 1) * hd, :] for r in range(r_heads)], axis=1)
        q128 = jnp.concatenate([q64, q64], axis=0)
        half = lax.broadcasted_iota(jnp.int32, q128.shape, 0) // hd
        return jnp.where(half == g, q128, jnp.zeros_like(q128))

    q_plain = stack_q(qpt_ref)
    q_rot = stack_q(qrt_ref)

    ncp = kc_ref.shape[0]
    start = pl.multiple_of(jnp.maximum(q0 + tq - wlen, 0), LANES)
    raw_c = _dot(kc_ref[...], q_plain)
    raw_w = _dot(kw_ref[pl.ds(start, wlen), :], q_rot)

    n_idx = lax.broadcasted_iota(jnp.int32, (ncp, tq), 0)
    c_ok = (n_idx * CMP_STRIDE + (CMP_BLOCK - 1)) <= qpos
    s_c = raw_c + heads(jnp.where(c_ok, 0.0, NEG_INF))
    e_c = jnp.exp2(s_c - jnp.max(s_c, axis=0, keepdims=True))
    col = lax.broadcasted_iota(jnp.int32, (1, r_heads * tq), 1)
    any_ok = jnp.where(q0 + col % tq >= CMP_BLOCK - 1, 1.0, 0.0)
    p_c = e_c * (any_ok / jnp.sum(e_c, axis=0, keepdims=True))
    o_c = _dot(vct_ref[0:hd, :], p_c.astype(BF16))

    p_sum = head_cols(p_c, 0)
    for r in range(1, r_heads):
        p_sum = p_sum + head_cols(p_c, r)
    ovlt = ovlt_ref[...]
    nb = ovlt.shape[0]
    imp, rem = None, p_sum
    for term in range(3):
        piece = rem.astype(BF16)
        d = _dot(ovlt, piece)
        imp = d if imp is None else imp + d
        rem = rem - piece.astype(F32)
    blk = lax.broadcasted_iota(jnp.int32, (nb, tq), 0)
    q_blk = qpos // SLC_BLOCK
    valid = blk <= q_blk
    forced = (blk == 0) | (blk == q_blk) | (blk == q_blk - 1)
    score = jnp.where(valid, imp + SLC_FORCED_BONUS * jnp.where(forced, 1.0, 0.0), -1.0)
    rank = jnp.zeros((nb, tq), F32)
    for jp in range(n_sel):
        other = score[jp:jp + 1, :]
        gt = jnp.where(other > score, 1.0, 0.0)
        ge = jnp.where(other >= score, 1.0, 0.0)
        rank = rank + jnp.where(blk > jp, ge, gt)
    bias_ref[...] = jnp.where(valid & (rank < float(top_n)), 0.0, NEG_INF)

    vwt = _with_sum_rows(jnp.concatenate([vwt_ref[start // LANES + j] for j in range(wlen // LANES)], axis=1))
    dist = qpos - (start + lax.broadcasted_iota(jnp.int32, (wlen, tq), 0))
    w_ok = (dist >= 0) & (dist < WINDOW)
    s_w = raw_w + heads(jnp.where(w_ok, 0.0, NEG_INF))
    e_w = jnp.exp2(s_w - jnp.max(s_w, axis=0, keepdims=True))
    o_w = _normalized(_dot(vwt, e_w.astype(BF16)), hd)

    gt_all = jax.nn.sigmoid(ng_ref[...]).T

    def gate(r, c):
        return gt_all[3 * r + c:3 * r + c + 1, :]

    for r in range(r_heads):
        part_ref[r * hd:(r + 1) * hd, :] = gate(r, 0) * head_cols(o_c, r) + gate(r, 2) * head_cols(o_w, r)

    m_ref[...] = jnp.full(m_ref.shape, NEG_INF, F32)
    acc_ref[...] = jnp.zeros(acc_ref.shape, F32)
    sub = tk // LANES
    blk_per_tile = tk // SLC_BLOCK

    def scores(kt, buf):
        s_ref, mx_ref = buf
        bias = jnp.concatenate(
            [jnp.broadcast_to(bias_ref[pl.ds(kt * blk_per_tile + i, 1), :], (SLC_BLOCK, tq))
             for i in range(blk_per_tile)], axis=0)
        kpos = kt * tk + lax.broadcasted_iota(jnp.int32, (tk, tq), 0)
        bias = jnp.where(kpos <= qpos, bias, NEG_INF)
        s = _dot(ks_ref[pl.ds(pl.multiple_of(kt * tk, tk), tk), :], q_rot) + heads(bias)
        s_ref[...] = s
        mx_ref[...] = jnp.max(s, axis=0, keepdims=True)

    def absorb(kt, buf):
        s_ref, mx_ref = buf
        vt = _with_sum_rows(jnp.concatenate([vst_ref[kt * sub + j] for j in range(sub)], axis=1))
        _online_update(s_ref[...], vt, m_ref, acc_ref, mx_ref[...])

    kt_last = q0 // tk
    buf_a, buf_b = (sa_ref, mxa_ref), (sb_ref, mxb_ref)
    scores(0, buf_a)

    def sel_body(j, carry):
        scores(2 * j + 1, buf_b)
        absorb(2 * j, buf_a)
        scores(2 * j + 2, buf_a)
        absorb(2 * j + 1, buf_b)
        return carry

    lax.fori_loop(0, kt_last // 2, sel_body, 0)

    @pl.when(kt_last % 2 == 0)
    def _():
        absorb(kt_last, buf_a)

    @pl.when(kt_last % 2 == 1)
    def _():
        scores(kt_last, buf_b)
        absorb(kt_last - 1, buf_a)
        absorb(kt_last, buf_b)

    o_s = _normalized(acc_ref[...], hd)
    outs = [part_ref[r * hd:(r + 1) * hd, :] + gate(r, 1) * head_cols(o_s, r) for r in range(r_heads)]
    o_ref[...] = jnp.concatenate(outs, axis=0).T.astype(BF16)


def _nsa_attn(nqpt, nqrt, cmp_kv, cmp_kvt, ks, vst, kw, vwt, z, ovlt, b, s, tk):
    tq = NSA_Q_TILE
    assert tk % tq == 0
    n_sel = s // SLC_BLOCK
    top_n = min(SLC_TOP_N, n_sel)
    wlen = min(WINDOW + tq, s)
    ncp = cmp_kv.shape[3]
    gw = NSA_GROUP_SIZE * NSA_HEAD_DIM
    nq = s // tq
    kernel = functools.partial(_nsa_kernel, tq=tq, tk=tk, n_sel=n_sel, top_n=top_n, wlen=wlen)
    qt_spec = pl.BlockSpec((tq // LANES, gw, LANES), lambda i, g, j: (i * nq + j, g, 0))
    k_spec = pl.BlockSpec((s, LANES), lambda i, g, j: (i, 0))
    dv = NSA_HEAD_DIM + SUM_ROWS
    vt_spec = pl.BlockSpec((None, s // LANES, NSA_HEAD_DIM, LANES), lambda i, g, j: (g, i, 0, 0))
    cols = NSA_GROUP_SIZE * tq
    return pl.pallas_call(
        kernel,
        out_shape=jax.ShapeDtypeStruct((b, s, NSA_W), BF16),
        grid=(b, NSA_KV_GROUPS, nq),
        in_specs=[
            qt_spec, qt_spec,
            pl.BlockSpec((None, None, None, ncp, LANES), lambda i, g, j: (0, i, g, 0, 0)),
            pl.BlockSpec((None, None, None, LANES, ncp), lambda i, g, j: (1, i, g, 0, 0)),
            k_spec, vt_spec, k_spec, vt_spec,
            pl.BlockSpec((tq, LANES), lambda i, g, j: (i * nq + j, Z_NG // LANES + g)),
            pl.BlockSpec(ovlt.shape, lambda i, g, j: (0, 0)),
        ],
        out_specs=pl.BlockSpec((None, tq, gw), lambda i, g, j: (i, j, g)),
        scratch_shapes=[pltpu.VMEM((1, cols), F32),
                        pltpu.VMEM((dv, cols), F32), pltpu.VMEM((gw, tq), F32),
                        pltpu.VMEM((n_sel, tq), F32),
                        pltpu.VMEM((tk, cols), F32), pltpu.VMEM((tk, cols), F32),
                        pltpu.VMEM((1, cols), F32), pltpu.VMEM((1, cols), F32)],
        compiler_params=_params(("parallel", "parallel", "arbitrary")),
        name="nsa_attn",
    )(nqpt, nqrt, cmp_kv, cmp_kvt, ks, vst, kw, vwt, z, ovlt)


def _merge_out_kernel(x_ref, ya_ref, yb_ref, ga_ref, gb_ref, wa_ref, wb_ref, wo_ref, o_ref, mg_ref,
                      *, tn):
    d = x_ref.shape[1]
    ya = ya_ref[...]
    yb = yb_ref[...]
    for c in range(0, d, tn):
        ta = _dot(ya, wa_ref[:, c:c + tn])
        tb = _dot(yb, wb_ref[:, c:c + tn])
        mg_ref[:, c:c + tn] = (jax.nn.sigmoid(ga_ref[:, c:c + tn]) * ta
                               + jax.nn.sigmoid(gb_ref[:, c:c + tn]) * tb).astype(BF16)
    mg = mg_ref[...]
    for c in range(0, d, tn):
        o_ref[:, c:c + tn] = x_ref[:, c:c + tn] + _dot(mg, wo_ref[:, c:c + tn])


def _merge_out(x2d, ya, yb, zg, wa, wb, wo, tm, tn):
    n, d = x2d.shape
    row = lambda i: (i, 0)
    const = lambda i: (0, 0)
    resident = pl.Buffered(1)
    return pl.pallas_call(
        functools.partial(_merge_out_kernel, tn=tn),
        out_shape=jax.ShapeDtypeStruct((n, d), F32),
        grid=(n // tm,),
        in_specs=[
            pl.BlockSpec((tm, d), row),
            pl.BlockSpec((tm, DIFF_W), row),
            pl.BlockSpec((tm, NSA_W), row),
            pl.BlockSpec((tm, d), lambda i: (i, 0)),
            pl.BlockSpec((tm, d), lambda i: (i, 1)),
            pl.BlockSpec((DIFF_W, d), const, pipeline_mode=resident),
            pl.BlockSpec((NSA_W, d), const, pipeline_mode=resident),
            pl.BlockSpec((d, d), const, pipeline_mode=resident),
        ],
        out_specs=pl.BlockSpec((tm, d), row),
        scratch_shapes=[pltpu.VMEM((tm, d), BF16)],
        compiler_params=_params(("parallel",)),
        name="merge_out",
    )(x2d, ya, yb, zg, zg, wa, wb, wo)


def _mlp_kernel(x_ref, g_ref, wu_ref, wd_ref, o_ref, h_ref, acc_ref):
    f = pl.program_id(1)

    @pl.when(f == 0)
    def _():
        h_ref[...] = (_rms(x_ref[...]) * g_ref[...]).astype(BF16)
        acc_ref[...] = jnp.zeros(acc_ref.shape, F32)

    u = jnp.maximum(_dot(h_ref[...], wu_ref[...]), 0.0)
    acc_ref[...] += _dot((u * u).astype(BF16), wd_ref[...])

    @pl.when(f == pl.num_programs(1) - 1)
    def _():
        o_ref[...] = x_ref[...] + acc_ref[...]


def _mlp(x2d, g, wu, wd, tm, tf):
    n, d = x2d.shape
    ff = wu.shape[1]
    return pl.pallas_call(
        _mlp_kernel,
        out_shape=jax.ShapeDtypeStruct((n, d), F32),
        grid=(n // tm, ff // tf),
        in_specs=[
            pl.BlockSpec((tm, d), lambda i, f: (i, 0)),
            pl.BlockSpec((1, d), lambda i, f: (0, 0)),
            pl.BlockSpec((d, tf), lambda i, f: (0, f)),
            pl.BlockSpec((tf, d), lambda i, f: (f, 0)),
        ],
        out_specs=pl.BlockSpec((tm, d), lambda i, f: (i, 0)),
        scratch_shapes=[pltpu.VMEM((tm, d), BF16), pltpu.VMEM((tm, d), F32)],
        compiler_params=_params(("parallel", "arbitrary")),
        name="mlp",
    )(x2d, g, wu, wd)


def _ple_kernel(x_ref, p_ref, g_ref, wp_ref, wg_ref, o_ref):
    x = x_ref[...]
    e = _rms(_dot(p_ref[...].astype(BF16), wp_ref[...])) * g_ref[...]
    gate = jax.nn.sigmoid(_dot(_rms(x).astype(BF16), wg_ref[...]))
    o_ref[...] = x + gate * e


def _ple(x2d, p2d, g, wp, wg, tm):
    n, d = x2d.shape
    row = lambda i: (i, 0)
    const = lambda i: (0, 0)
    resident = pl.Buffered(1)
    return pl.pallas_call(
        _ple_kernel,
        out_shape=jax.ShapeDtypeStruct((n, d), F32),
        grid=(n // tm,),
        in_specs=[
            pl.BlockSpec((tm, d), row),
            pl.BlockSpec((tm, PLE_DIM), row),
            pl.BlockSpec((1, d), const),
            pl.BlockSpec((PLE_DIM, d), const, pipeline_mode=resident),
            pl.BlockSpec((d, d), const, pipeline_mode=resident),
        ],
        out_specs=pl.BlockSpec((tm, d), row),
        compiler_params=_params(("parallel",)),
        name="ple",
    )(x2d, p2d, g, wp, wg)


def _rope_tables(positions):
    rot = 2 * ROT_HALF
    inv_freq = jnp.power(ROPE_THETA, -jnp.arange(0, rot, 2, dtype=F32) / rot)
    ang = positions.astype(F32)[..., None] * inv_freq
    cos, sin = jnp.cos(ang), jnp.sin(ang)
    n = cos.shape[0] * cos.shape[1]
    cos, sin = cos.reshape(n, ROT_HALF), sin.reshape(n, ROT_HALF)
    rest = NSA_HEAD_DIM - rot
    ones, zeros_r, zeros_h = jnp.ones((n, rest), F32), jnp.zeros((n, rest), F32), jnp.zeros((n, ROT_HALF), F32)
    ra = jnp.concatenate([cos, cos, ones], axis=1)
    rb = jnp.concatenate([-sin, zeros_h, zeros_r], axis=1)
    rc = jnp.concatenate([zeros_h, sin, zeros_r], axis=1)
    return tuple(jnp.tile(t, (1, LANES // NSA_HEAD_DIM)) for t in (ra, rb, rc))


def _regroup_w_in(w_in):
    d = w_in.shape[0]
    ng0 = 3 * DIFF_W + NSA_W + 6 * NSA_KV_W
    per_g = 3 * NSA_GROUP_SIZE
    parts = [w_in[:, :ng0]]
    for g in range(NSA_KV_GROUPS):
        parts += [w_in[:, ng0 + g * per_g:ng0 + (g + 1) * per_g], jnp.zeros((d, LANES - per_g), w_in.dtype)]
    w_main = jnp.concatenate(parts, axis=1).astype(BF16)
    w_gate = w_in[:, ng0 + 3 * NSA_HEADS:].astype(BF16)
    return w_main, w_gate


def _pick(n, pref):
    t = min(pref, n)
    while n % t:
        t //= 2
    return t


def kernel(x, p, positions, norm_mix, w_in, diff_q_norm, diff_k_norm, diff_lambda, diff_subln,
           nsa_q_norm, nsa_k_norm, cmp_pos, cmp_w1, cmp_w2, w_proj_diff, w_proj_nsa, w_out,
           norm_mlp, w_mlp_up, w_mlp_down, w_ple_proj, norm_ple, w_ple_gate):
    b, s, d = x.shape
    n = b * s
    assert d == D_MODEL and s % 512 == 0
    ra, rb, rc = _rope_tables(positions)
    seg = (jnp.arange(LANES)[:, None] // NSA_HEAD_DIM == jnp.arange(LANES)[None, :] // NSA_HEAD_DIM).astype(BF16)
    ncp = s // CMP_STRIDE
    n_sel = s // SLC_BLOCK
    assert n_sel % 16 == 0 and n_sel <= LANES
    cs = jnp.arange(ncp)[None, :] * CMP_STRIDE
    ss = jnp.arange(n_sel)[:, None] * SLC_BLOCK
    ovlt = ((cs < ss + SLC_BLOCK) & (cs + CMP_BLOCK > ss)).astype(BF16)
    tk_nsa, t_diff = 512, 512

    x2d = x.reshape(n, d)
    for i in range(p.shape[0]):
        lambda_init = 0.8 - 0.6 * math.exp(-0.3 * i)
        dup = lambda v: jnp.tile(v, LANES // NSA_HEAD_DIM)
        gains = jnp.zeros((8, LANES), F32).at[0].set(dup(diff_q_norm[i])).at[1].set(dup(diff_k_norm[i]))
        gains = gains.at[2].set(dup(nsa_q_norm[i])).at[3].set(dup(nsa_k_norm[i]))

        w_main, w_gate = _regroup_w_in(w_in[i])
        z = _in_proj(x2d, norm_mix[i][None], w_main, _pick(n, 1024), 1024)
        zg = _in_proj(x2d, norm_mix[i][None], w_gate, _pick(n, 1024), 1024)
        qdt, kd, vdt, nqpt, nqrt, ks, vst, kw, vwt, xc = _prep(z, ra, rb, rc, gains, seg,
                                                               _pick(t_diff, 256), t_diff)

        half = CMP_STRIDE * NSA_HEAD_DIM
        cmp_kv, cmp_kvt = _compress(
            xc,
            cmp_pos[i].reshape(2, 2, half),
            cmp_w1[i].reshape(2, 2, half, CMP_HIDDEN).astype(BF16),
            jnp.tile(cmp_w2[i], (1, 1, LANES // NSA_HEAD_DIM)).astype(BF16),
            gains[3:4], b)

        ya = _diff_attn(diff_lambda[i], qdt, kd.reshape(b, s, DIFF_W), vdt, diff_subln[i][None],
                        lambda_init, t_diff)
        yb = _nsa_attn(nqpt, nqrt, cmp_kv, cmp_kvt, ks, vst, kw, vwt, z, ovlt, b, s, tk_nsa)

        x2d = _merge_out(x2d, ya.reshape(n, DIFF_W), yb.reshape(n, NSA_W), zg,
                         w_proj_diff[i].astype(BF16), w_proj_nsa[i].astype(BF16), w_out[i].astype(BF16),
                         _pick(n, 256), 512)
        x2d = _mlp(x2d, norm_mlp[i][None], w_mlp_up[i].astype(BF16), w_mlp_down[i].astype(BF16),
                   _pick(n, 512), 1024)
        x2d = _ple(x2d, p[i].reshape(n, PLE_DIM), norm_ple[i][None], w_ple_proj[i].astype(BF16),
                   w_ple_gate[i].astype(BF16), _pick(n, 512))
    return x2d.reshape(b, s, d)
```

```python
import functools
import math
from typing import NamedTuple

import jax
import jax.numpy as jnp
from jax import lax
from jax.experimental import pallas as pl
from jax.experimental.pallas import tpu as pltpu

F32 = jnp.float32
BF16 = jnp.bfloat16

D_MODEL = 2048
PLE_DIM = 256
ROPE_THETA = 500000.0
ROPE_FRACTION = 4
NORM_EPS = 1e-6
NEG_INF = -1e30
DIFF_HEADS = 8
DIFF_SUB_DIM = 64
DIFF_V_DIM = 2 * DIFF_SUB_DIM
NSA_HEADS = 16
NSA_KV_GROUPS = 2
NSA_GROUP_SIZE = NSA_HEADS // NSA_KV_GROUPS
NSA_HEAD_DIM = 64
CMP_BLOCK = 32
CMP_STRIDE = 16
CMP_HIDDEN = 256
SLC_BLOCK = 64
SLC_TOP_N = 16
SLC_FORCED_BONUS = 1e4
WINDOW = 512
DIFF_W = DIFF_HEADS * DIFF_V_DIM
NSA_W = NSA_HEADS * NSA_HEAD_DIM
NSA_KV_W = NSA_KV_GROUPS * NSA_HEAD_DIM
ROT_HALF = NSA_HEAD_DIM // ROPE_FRACTION // 2
QK_SCALE = NSA_HEAD_DIM ** -0.5
Q_SCALE = QK_SCALE * math.log2(math.e)
SUM_ROWS = 16

LANES = 128
VMEM_LIMIT = 56 * 1024 * 1024
NSA_Q_TILE = 2 * LANES

Z_DQ, Z_DK, Z_DV, Z_NQ = 0, 1024, 2048, 3072
Z_KC, Z_VC, Z_KS, Z_VS, Z_KW, Z_VW = 4096, 4224, 4352, 4480, 4608, 4736
PREP_W = 4864
Z_NG = PREP_W


def _params(sem):
    return pltpu.CompilerParams(dimension_semantics=sem, vmem_limit_bytes=VMEM_LIMIT)


class _Tiles(NamedTuple):
    proj_m: int
    proj_n: int
    prep_m: int
    attn: int
    merge_m: int
    merge_n: int
    mlp_m: int
    mlp_f: int
    ple_m: int


def _pick(n, pref):
    t = min(pref, n)
    while n % t:
        t //= 2
    return t


def _tiles(n):
    attn = 512
    return _Tiles(proj_m=_pick(n, 1024), proj_n=1024, prep_m=_pick(attn, 256), attn=attn,
                  merge_m=_pick(n, 256), merge_n=512, mlp_m=_pick(n, 512), mlp_f=1024, ple_m=_pick(n, 512))


def _dot(a, b):
    return jnp.dot(a, b, preferred_element_type=F32)


def _rms(x, eps=NORM_EPS):
    return x * lax.rsqrt(jnp.mean(x * x, axis=-1, keepdims=True) + eps)


def _split_dot(a, m_bf16, terms):
    acc = None
    rem = a
    for t in range(terms):
        piece = rem.astype(BF16)
        d = _dot(piece, m_bf16)
        acc = d if acc is None else acc + d
        if t + 1 < terms:
            rem = rem - piece.astype(F32)
    return acc


def _in_proj_kernel(x_ref, g_ref, w_ref, o_ref, h_ref):
    @pl.when(pl.program_id(1) == 0)
    def _():
        h_ref[...] = (_rms(x_ref[...]) * g_ref[...]).astype(BF16)

    o_ref[...] = _dot(h_ref[...], w_ref[...])


def _in_proj(x2d, g, w, tm, tn):
    n, d = x2d.shape
    nout = w.shape[1]
    return pl.pallas_call(
        _in_proj_kernel,
        out_shape=jax.ShapeDtypeStruct((n, nout), F32),
        grid=(n // tm, nout // tn),
        in_specs=[
            pl.BlockSpec((tm, d), lambda i, j: (i, 0)),
            pl.BlockSpec((1, d), lambda i, j: (0, 0)),
            pl.BlockSpec((d, tn), lambda i, j: (0, j)),
        ],
        out_specs=pl.BlockSpec((tm, tn), lambda i, j: (i, j)),
        scratch_shapes=[pltpu.VMEM((tm, d), BF16)],
        compiler_params=_params(("parallel", "arbitrary")),
        name="in_proj",
    )(x2d, g, w)


def _prep_kernel(z_ref, ra_ref, rb_ref, rc_ref, gain_ref, seg_ref,
                 qdt_ref, kd_ref, vdt_ref, nqpt_ref, nqrt_ref, ks_ref, vst_ref, kw_ref, vwt_ref, xc_ref,
                 stage_ref):
    ra, rb, rc = ra_ref[...], rb_ref[...], rc_ref[...]
    seg = seg_ref[...]
    tm = ra.shape[0]
    nt = tm // LANES

    def rope(y):
        return (y * ra + pltpu.roll(y, LANES - ROT_HALF, 1) * rb + pltpu.roll(y, ROT_HALF, 1) * rc)

    def head_norm(xb, gain):
        ss = _split_dot(xb * xb, seg, 2)
        return xb * lax.rsqrt(ss * (1.0 / NSA_HEAD_DIM) + NORM_EPS) * gain

    def zblk(col):
        return z_ref[:, col:col + LANES]

    for hb in range(DIFF_W // LANES):
        c = hb * LANES
        qd = rope(head_norm(zblk(Z_DQ + c), gain_ref[0:1, :])) * Q_SCALE
        qdt_ref[c:c + LANES, :] = qd.T.astype(BF16)
        kd_ref[:, c:c + LANES] = rope(head_norm(zblk(Z_DK + c), gain_ref[1:2, :])).astype(BF16)
        vdt_ref[c:c + LANES, :] = zblk(Z_DV + c).T.astype(BF16)
        yq = head_norm(zblk(Z_NQ + c), gain_ref[2:3, :])
        qpt = (yq * Q_SCALE).T.astype(BF16)
        qrt = (rope(yq) * Q_SCALE).T.astype(BF16)
        for j in range(nt):
            nqpt_ref[j, c:c + LANES, :] = qpt[:, j * LANES:(j + 1) * LANES]
            nqrt_ref[j, c:c + LANES, :] = qrt[:, j * LANES:(j + 1) * LANES]

    ks_ref[...] = rope(head_norm(zblk(Z_KS), gain_ref[3:4, :])).astype(BF16)
    kw_ref[...] = rope(head_norm(zblk(Z_KW), gain_ref[3:4, :])).astype(BF16)
    hd = NSA_HEAD_DIM
    for col, v_out in ((Z_VS, vst_ref), (Z_VW, vwt_ref)):
        vt = zblk(col).T.astype(BF16)
        for g in range(NSA_KV_GROUPS):
            for j in range(nt):
                v_out[g, j] = vt[g * hd:(g + 1) * hd, j * LANES:(j + 1) * LANES]

    nchunk = tm // CMP_STRIDE
    lo = lax.broadcasted_iota(jnp.int32, (nchunk, LANES), 1) < hd
    for kv, col in enumerate((Z_KC, Z_VC)):
        stage_ref[...] = zblk(col)
        for t in range(0, CMP_STRIDE, 2):
            a = stage_ref[pl.ds(t, nchunk, stride=CMP_STRIDE), :]
            b = stage_ref[pl.ds(t + 1, nchunk, stride=CMP_STRIDE), :]
            xc_ref[kv, 0, :, t * hd:(t + 2) * hd] = jnp.where(lo, a, pltpu.roll(b, hd, 1))
            xc_ref[kv, 1, :, t * hd:(t + 2) * hd] = jnp.where(lo, pltpu.roll(a, hd, 1), b)


def _prep(z, ra, rb, rc, gains, seg, tm, t_diff):
    n = z.shape[0]
    nt = tm // LANES
    per = t_diff // tm
    row = lambda i: (i, 0)
    tile_t = lambda i: (i // per, 0, i % per)
    diff_t = jax.ShapeDtypeStruct((n // t_diff, DIFF_W, t_diff), BF16)
    nsa_qt = jax.ShapeDtypeStruct((n // LANES, NSA_W, LANES), BF16)
    nsa_vt = jax.ShapeDtypeStruct((NSA_KV_GROUPS, n // LANES, NSA_HEAD_DIM, LANES), BF16)
    diff_t_spec = pl.BlockSpec((None, DIFF_W, tm), tile_t)
    nsa_qt_spec = pl.BlockSpec((nt, NSA_W, LANES), lambda i: (i, 0, 0))
    nsa_vt_spec = pl.BlockSpec((NSA_KV_GROUPS, nt, NSA_HEAD_DIM, LANES), lambda i: (0, i, 0, 0))
    chunk_w = CMP_STRIDE * NSA_HEAD_DIM
    chunks = jax.ShapeDtypeStruct((2, NSA_KV_GROUPS, n // CMP_STRIDE, chunk_w), F32)
    chunks_spec = pl.BlockSpec((2, NSA_KV_GROUPS, tm // CMP_STRIDE, chunk_w), lambda i: (0, 0, i, 0))
    return pl.pallas_call(
        _prep_kernel,
        out_shape=[diff_t, jax.ShapeDtypeStruct((n, DIFF_W), BF16), diff_t, nsa_qt, nsa_qt,
                   jax.ShapeDtypeStruct((n, LANES), BF16), nsa_vt,
                   jax.ShapeDtypeStruct((n, LANES), BF16), nsa_vt, chunks],
        grid=(n // tm,),
        in_specs=[
            pl.BlockSpec((tm, PREP_W), row),
            pl.BlockSpec((tm, LANES), row),
            pl.BlockSpec((tm, LANES), row),
            pl.BlockSpec((tm, LANES), row),
            pl.BlockSpec((8, LANES), lambda i: (0, 0)),
            pl.BlockSpec((LANES, LANES), lambda i: (0, 0)),
        ],
        out_specs=[diff_t_spec, pl.BlockSpec((tm, DIFF_W), row), diff_t_spec, nsa_qt_spec, nsa_qt_spec,
                   pl.BlockSpec((tm, LANES), row), nsa_vt_spec,
                   pl.BlockSpec((tm, LANES), row), nsa_vt_spec, chunks_spec],
        scratch_shapes=[pltpu.VMEM((tm, LANES), F32)],
        compiler_params=_params(("parallel",)),
        name="prep",
    )(z, ra, rb, rc, gains, seg)


def _compress_kernel(x_ref, pos_ref, w1_ref, w2_ref, gain_ref, o_ref, ot_ref):
    x = x_ref[...]
    nc = x.shape[0]
    a = (x + pos_ref[0:1, :]).astype(BF16)
    b = (x + pos_ref[1:2, :]).astype(BF16)
    u = _dot(a, w1_ref[0])
    v = _dot(b, w1_ref[1])
    pre = u + pltpu.roll(v, nc - 1, 0)
    cdf = 0.5 * (1.0 + jnp.tanh(math.sqrt(2.0 / math.pi) * (pre + 0.044715 * (pre * pre * pre))))
    c = _dot((pre * cdf).astype(BF16), w2_ref[...])
    is_key = pl.program_id(0) == 0
    out = jnp.where(is_key, _rms(c) * gain_ref[...], c)
    o_ref[...] = out.astype(BF16)
    ot_ref[...] = out.T.astype(BF16)


def _compress(xc, pos, w1, w2d, gain, b):
    _, g, n_chunks, cw = xc.shape
    nc = n_chunks // b
    return pl.pallas_call(
        _compress_kernel,
        out_shape=[jax.ShapeDtypeStruct((2, b, g, nc, LANES), BF16),
                   jax.ShapeDtypeStruct((2, b, g, LANES, nc), BF16)],
        grid=(2, b, g),
        in_specs=[
            pl.BlockSpec((None, None, nc, cw), lambda t, i, j: (t, j, i, 0)),
            pl.BlockSpec((None, 2, cw), lambda t, i, j: (t, 0, 0)),
            pl.BlockSpec((None, 2, cw, CMP_HIDDEN), lambda t, i, j: (t, 0, 0, 0)),
            pl.BlockSpec((None, CMP_HIDDEN, LANES), lambda t, i, j: (t, 0, 0)),
            pl.BlockSpec((1, LANES), lambda t, i, j: (0, 0)),
        ],
        out_specs=[pl.BlockSpec((None, None, None, nc, LANES), lambda t, i, j: (t, i, j, 0, 0)),
                   pl.BlockSpec((None, None, None, LANES, nc), lambda t, i, j: (t, i, j, 0, 0))],
        compiler_params=_params(("parallel", "parallel", "parallel")),
        name="compress",
    )(xc, pos, w1, w2d, gain)


def _online_update(s, vt, m_ref, acc_ref, tile_max=None):
    m_old = m_ref[...]
    tile_max = jnp.max(s, axis=0, keepdims=True) if tile_max is None else tile_max
    m_new = jnp.maximum(m_old, tile_max)
    p = jnp.exp2(s - m_new)
    acc_ref[...] = jnp.exp2(m_old - m_new) * acc_ref[...] + _dot(vt, p.astype(BF16))
    m_ref[...] = m_new


def _with_sum_rows(vt):
    rows = lax.broadcasted_iota(jnp.int32, (SUM_ROWS, vt.shape[1]), 0)
    return jnp.concatenate([vt, jnp.where(rows == 0, 1.0, 0.0).astype(vt.dtype)], axis=0)


def _normalized(acc, dv):
    return acc[0:dv, :] / acc[dv:dv + 1, :]


def _diff_attn_kernel(lam_ref, qt_ref, k_ref, vt_ref, g_ref, o_ref,
                      m1, a1, m2, a2, sa_ref, sb_ref, mxa_ref, mxb_ref, *, lambda_init, t):
    qi = pl.program_id(2)
    nq = qt_ref.shape[0]

    def sub_heads(tile):
        qt = qt_ref[tile]
        row = lax.broadcasted_iota(jnp.int32, qt.shape, 0)
        zero = jnp.zeros_like(qt)
        return jnp.where(row < DIFF_SUB_DIM, qt, zero), jnp.where(row >= DIFF_SUB_DIM, qt, zero)

    q_sub = sub_heads(qi)
    stats = ((m1, a1), (m2, a2))
    for m_ref, a_ref in stats:
        m_ref[...] = jnp.full(m_ref.shape, NEG_INF, F32)
        a_ref[...] = jnp.zeros(a_ref.shape, F32)

    def scores(kt, buf, q_pair=q_sub):
        s_ref, mx_ref = buf
        k = k_ref[pl.ds(pl.multiple_of(kt * t, t), t), :]
        for i in range(2):
            s = _dot(k, q_pair[i])
            s_ref[i] = s
            mx_ref[i] = jnp.max(s, axis=0, keepdims=True)

    def absorb(kt, buf, causal):
        s_ref, mx_ref = buf
        vt = _with_sum_rows(vt_ref[kt])
        for i in range(2):
            s = s_ref[i]
            if causal:
                kpos = lax.broadcasted_iota(jnp.int32, s.shape, 0)
                qpos = lax.broadcasted_iota(jnp.int32, s.shape, 1)
                _online_update(jnp.where(kpos <= qpos, s, NEG_INF), vt, *stats[i])
            else:
                _online_update(s, vt, *stats[i], mx_ref[i])

    buf_a, buf_b = (sa_ref, mxa_ref), (sb_ref, mxb_ref)

    @pl.when(qi == 0)
    def _():
        scores(0, buf_a)

    def body(j, carry):
        scores(2 * j + 1, buf_b)
        absorb(2 * j, buf_a, False)
        scores(2 * j + 2, buf_a)
        absorb(2 * j + 1, buf_b, False)
        return carry

    lax.fori_loop(0, qi // 2, body, 0)

    @pl.when(qi % 2 == 0)
    def _():
        absorb(qi, buf_a, True)

    @pl.when(qi % 2 == 1)
    def _():
        scores(qi, buf_b)
        absorb(qi - 1, buf_a, False)
        absorb(qi, buf_b, True)

    lp = lam_ref[...]
    lam = (jnp.exp(jnp.sum(lp[0:1] * lp[1:2], axis=-1, keepdims=True))
           - jnp.exp(jnp.sum(lp[2:3] * lp[3:4], axis=-1, keepdims=True)) + lambda_init)
    o = (_normalized(a1[...], DIFF_V_DIM) - lam * _normalized(a2[...], DIFF_V_DIM)).T
    o_ref[...] = (_rms(o) * g_ref[...] * (1.0 - lambda_init)).astype(BF16)

    scores(0, buf_a, sub_heads(jnp.minimum(qi + 1, nq - 1)))


def _diff_attn(lam_p, qdt, kd, vdt, subln, lambda_init, t):
    b, s, _ = kd.shape
    nq = s // t
    kernel = functools.partial(_diff_attn_kernel, lambda_init=lambda_init, t=t)
    dv = DIFF_V_DIM + SUM_ROWS
    stat = pltpu.VMEM((1, t), F32)
    acc = pltpu.VMEM((dv, t), F32)
    score_buf = pltpu.VMEM((2, t, t), F32)
    return pl.pallas_call(
        kernel,
        out_shape=jax.ShapeDtypeStruct((b, s, DIFF_W), BF16),
        grid=(b, DIFF_HEADS, nq),
        in_specs=[
            pl.BlockSpec((4, DIFF_SUB_DIM), lambda i, h, j: (0, 0)),
            pl.BlockSpec((nq, LANES, t), lambda i, h, j: (i, h, 0)),
            pl.BlockSpec((None, s, LANES), lambda i, h, j: (i, 0, h)),
            pl.BlockSpec((nq, LANES, t), lambda i, h, j: (i, h, 0)),
            pl.BlockSpec((1, LANES), lambda i, h, j: (0, 0)),
        ],
        out_specs=pl.BlockSpec((None, t, LANES), lambda i, h, j: (i, j, h)),
        scratch_shapes=[stat, acc, stat, acc, score_buf, score_buf,
                        pltpu.VMEM((2, 1, t), F32), pltpu.VMEM((2, 1, t), F32)],
        compiler_params=_params(("parallel", "parallel", "arbitrary")),
        name="diff_attn",
    )(lam_p, qdt, kd, vdt, subln)


def _nsa_kernel(qpt_ref, qrt_ref, kc_ref, vct_ref, ks_ref, vst_ref, kw_ref, vwt_ref, ng_ref,
                ovlt_ref, o_ref, m_ref, acc_ref, part_ref, bias_ref, sa_ref, sb_ref, mxa_ref, mxb_ref,
                *, tq, tk, n_sel, top_n, wlen):
    r_heads = NSA_GROUP_SIZE
    hd = NSA_HEAD_DIM
    g = pl.program_id(1)
    q0 = pl.program_id(2) * tq
    qpos = q0 + lax.broadcasted_iota(jnp.int32, (1, tq), 1)

    def heads(x):
        return jnp.concatenate([x] * r_heads, axis=1)

    def head_cols(x, r):
        return x[:, r * tq:(r + 1) * tq]

    def stack_q(qt_ref):
        qt = jnp.concatenate([qt_ref[t] for t in range(tq // LANES)], axis=1)
        q64 = jnp.concatenate([qt[r * hd:(r + 1) * hd, :] for r in range(r_heads)], axis=1)
        q128 = jnp.concatenate([q64, q64], axis=0)
        half = lax.broadcasted_iota(jnp.int32, q128.shape, 0) // hd
        return jnp.where(half == g, q128, jnp.zeros_like(q128))

    q_plain = stack_q(qpt_ref)
    q_rot = stack_q(qrt_ref)

    ncp = kc_ref.shape[0]
    start = pl.multiple_of(jnp.maximum(q0 + tq - wlen, 0), LANES)
    raw_c = _dot(kc_ref[...], q_plain)
    raw_w = _dot(kw_ref[pl.ds(start, wlen), :], q_rot)

    n_idx = lax.broadcasted_iota(jnp.int32, (ncp, tq), 0)
    c_ok = (n_idx * CMP_STRIDE + (CMP_BLOCK - 1)) <= qpos
    s_c = raw_c + heads(jnp.where(c_ok, 0.0, NEG_INF))
    e_c = jnp.exp2(s_c - jnp.max(s_c, axis=0, keepdims=True))
    col = lax.broadcasted_iota(jnp.int32, (1, r_heads * tq), 1)
    any_ok = jnp.where(q0 + col % tq >= CMP_BLOCK - 1, 1.0, 0.0)
    p_c = e_c * (any_ok / jnp.sum(e_c, axis=0, keepdims=True))
    o_c = _dot(vct_ref[0:hd, :], p_c.astype(BF16))

    p_sum = head_cols(p_c, 0)
    for r in range(1, r_heads):
        p_sum = p_sum + head_cols(p_c, r)
    ovlt = ovlt_ref[...]
    nb = ovlt.shape[0]
    imp, rem = None, p_sum
    for term in range(3):
        piece = rem.astype(BF16)
        d = _dot(ovlt, piece)
        imp = d if imp is None else imp + d
        rem = rem - piece.astype(F32)
    blk = lax.broadcasted_iota(jnp.int32, (nb, tq), 0)
    q_blk = qpos // SLC_BLOCK
    valid = blk <= q_blk
    forced = (blk == 0) | (blk == q_blk) | (blk == q_blk - 1)
    score = jnp.where(valid, imp + SLC_FORCED_BONUS * jnp.where(forced, 1.0, 0.0), -1.0)
    rank = jnp.zeros((nb, tq), F32)
    for jp in range(n_sel):
        other = score[jp:jp + 1, :]
        gt = jnp.where(other > score, 1.0, 0.0)
        ge = jnp.where(other >= score, 1.0, 0.0)
        rank = rank + jnp.where(blk > jp, ge, gt)
    bias_ref[...] = jnp.where(valid & (rank < float(top_n)), 0.0, NEG_INF)

    vwt = _with_sum_rows(jnp.concatenate([vwt_ref[start // LANES + j] for j in range(wlen // LANES)], axis=1))
    dist = qpos - (start + lax.broadcasted_iota(jnp.int32, (wlen, tq), 0))
    w_ok = (dist >= 0) & (dist < WINDOW)
    s_w = raw_w + heads(jnp.where(w_ok, 0.0, NEG_INF))
    e_w = jnp.exp2(s_w - jnp.max(s_w, axis=0, keepdims=True))
    o_w = _normalized(_dot(vwt, e_w.astype(BF16)), hd)

    gt_all = jax.nn.sigmoid(ng_ref[...]).T

    def gate(r, c):
        return gt_all[3 * r + c:3 * r + c + 1, :]

    for r in range(r_heads):
        part_ref[r * hd:(r + 1) * hd, :] = gate(r, 0) * head_cols(o_c, r) + gate(r, 2) * head_cols(o_w, r)

    m_ref[...] = jnp.full(m_ref.shape, NEG_INF, F32)
    acc_ref[...] = jnp.zeros(acc_ref.shape, F32)
    sub = tk // LANES
    blk_per_tile = tk // SLC_BLOCK

    def scores(kt, buf):
        s_ref, mx_ref = buf
        bias = jnp.concatenate(
            [jnp.broadcast_to(bias_ref[pl.ds(kt * blk_per_tile + i, 1), :], (SLC_BLOCK, tq))
             for i in range(blk_per_tile)], axis=0)
        kpos = kt * tk + lax.broadcasted_iota(jnp.int32, (tk, tq), 0)
        bias = jnp.where(kpos <= qpos, bias, NEG_INF)
        s = _dot(ks_ref[pl.ds(pl.multiple_of(kt * tk, tk), tk), :], q_rot) + heads(bias)
        s_ref[...] = s
        mx_ref[...] = jnp.max(s, axis=0, keepdims=True)

    def absorb(kt, buf):
        s_ref, mx_ref = buf
        vt = _with_sum_rows(jnp.concatenate([vst_ref[kt * sub + j] for j in range(sub)], axis=1))
        _online_update(s_ref[...], vt, m_ref, acc_ref, mx_ref[...])

    kt_last = q0 // tk
    buf_a, buf_b = (sa_ref, mxa_ref), (sb_ref, mxb_ref)
    scores(0, buf_a)

    def sel_body(j, carry):
        scores(2 * j + 1, buf_b)
        absorb(2 * j, buf_a)
        scores(2 * j + 2, buf_a)
        absorb(2 * j + 1, buf_b)
        return carry

    lax.fori_loop(0, kt_last // 2, sel_body, 0)

    @pl.when(kt_last % 2 == 0)
    def _():
        absorb(kt_last, buf_a)

    @pl.when(kt_last % 2 == 1)
    def _():
        scores(kt_last, buf_b)
        absorb(kt_last - 1, buf_a)
        absorb(kt_last, buf_b)

    o_s = _normalized(acc_ref[...], hd)
    outs = [part_ref[r * hd:(r + 1) * hd, :] + gate(r, 1) * head_cols(o_s, r) for r in range(r_heads)]
    o_ref[...] = jnp.concatenate(outs, axis=0).T.astype(BF16)


def _nsa_attn(nqpt, nqrt, cmp_kv, cmp_kvt, ks, vst, kw, vwt, z, ovlt, b, s, tk):
    tq = NSA_Q_TILE
    assert tk % tq == 0
    n_sel = s // SLC_BLOCK
    top_n = min(SLC_TOP_N, n_sel)
    wlen = min(WINDOW + tq, s)
    ncp = cmp_kv.shape[3]
    gw = NSA_GROUP_SIZE * NSA_HEAD_DIM
    nq = s // tq
    kernel = functools.partial(_nsa_kernel, tq=tq, tk=tk, n_sel=n_sel, top_n=top_n, wlen=wlen)
    qt_spec = pl.BlockSpec((tq // LANES, gw, LANES), lambda i, g, j: (i * nq + j, g, 0))
    k_spec = pl.BlockSpec((s, LANES), lambda i, g, j: (i, 0))
    dv = NSA_HEAD_DIM + SUM_ROWS
    vt_spec = pl.BlockSpec((None, s // LANES, NSA_HEAD_DIM, LANES), lambda i, g, j: (g, i, 0, 0))
    cols = NSA_GROUP_SIZE * tq
    return pl.pallas_call(
        kernel,
        out_shape=jax.ShapeDtypeStruct((b, s, NSA_W), BF16),
        grid=(b, NSA_KV_GROUPS, nq),
        in_specs=[
            qt_spec, qt_spec,
            pl.BlockSpec((None, None, None, ncp, LANES), lambda i, g, j: (0, i, g, 0, 0)),
            pl.BlockSpec((None, None, None, LANES, ncp), lambda i, g, j: (1, i, g, 0, 0)),
            k_spec, vt_spec, k_spec, vt_spec,
            pl.BlockSpec((tq, LANES), lambda i, g, j: (i * nq + j, Z_NG // LANES + g)),
            pl.BlockSpec(ovlt.shape, lambda i, g, j: (0, 0)),
        ],
        out_specs=pl.BlockSpec((None, tq, gw), lambda i, g, j: (i, j, g)),
        scratch_shapes=[pltpu.VMEM((1, cols), F32),
                        pltpu.VMEM((dv, cols), F32), pltpu.VMEM((gw, tq), F32),
                        pltpu.VMEM((n_sel, tq), F32),
                        pltpu.VMEM((tk, cols), F32), pltpu.VMEM((tk, cols), F32),
                        pltpu.VMEM((1, cols), F32), pltpu.VMEM((1, cols), F32)],
        compiler_params=_params(("parallel", "parallel", "arbitrary")),
        name="nsa_attn",
    )(nqpt, nqrt, cmp_kv, cmp_kvt, ks, vst, kw, vwt, z, ovlt)


def _merge_out_kernel(x_ref, ya_ref, yb_ref, ga_ref, gb_ref, wa_ref, wb_ref, wo_ref, o_ref, mg_ref,
                      *, tn):
    d = x_ref.shape[1]
    ya = ya_ref[...]
    yb = yb_ref[...]
    for c in range(0, d, tn):
        ta = _dot(ya, wa_ref[:, c:c + tn])
        tb = _dot(yb, wb_ref[:, c:c + tn])
        mg_ref[:, c:c + tn] = (jax.nn.sigmoid(ga_ref[:, c:c + tn]) * ta
                               + jax.nn.sigmoid(gb_ref[:, c:c + tn]) * tb).astype(BF16)
    mg = mg_ref[...]
    for c in range(0, d, tn):
        o_ref[:, c:c + tn] = x_ref[:, c:c + tn] + _dot(mg, wo_ref[:, c:c + tn])


def _merge_out(x2d, ya, yb, zg, wa, wb, wo, tm, tn):
    n, d = x2d.shape
    row = lambda i: (i, 0)
    const = lambda i: (0, 0)
    resident = pl.Buffered(1)
    return pl.pallas_call(
        functools.partial(_merge_out_kernel, tn=tn),
        out_shape=jax.ShapeDtypeStruct((n, d), F32),
        grid=(n // tm,),
        in_specs=[
            pl.BlockSpec((tm, d), row),
            pl.BlockSpec((tm, DIFF_W), row),
            pl.BlockSpec((tm, NSA_W), row),
            pl.BlockSpec((tm, d), lambda i: (i, 0)),
            pl.BlockSpec((tm, d), lambda i: (i, 1)),
            pl.BlockSpec((DIFF_W, d), const, pipeline_mode=resident),
            pl.BlockSpec((NSA_W, d), const, pipeline_mode=resident),
            pl.BlockSpec((d, d), const, pipeline_mode=resident),
        ],
        out_specs=pl.BlockSpec((tm, d), row),
        scratch_shapes=[pltpu.VMEM((tm, d), BF16)],
        compiler_params=_params(("parallel",)),
        name="merge_out",
    )(x2d, ya, yb, zg, zg, wa, wb, wo)


def _mlp_kernel(x_ref, g_ref, wu_ref, wd_ref, o_ref, h_ref, acc_ref):
    f = pl.program_id(1)

    @pl.when(f == 0)
    def _():
        h_ref[...] = (_rms(x_ref[...]) * g_ref[...]).astype(BF16)
        acc_ref[...] = jnp.zeros(acc_ref.shape, F32)

    u = jnp.maximum(_dot(h_ref[...], wu_ref[...]), 0.0)
    acc_ref[...] += _dot((u * u).astype(BF16), wd_ref[...])

    @pl.when(f == pl.num_programs(1) - 1)
    def _():
        o_ref[...] = x_ref[...] + acc_ref[...]


def _mlp(x2d, g, wu, wd, tm, tf):
    n, d = x2d.shape
    ff = wu.shape[1]
    return pl.pallas_call(
        _mlp_kernel,
        out_shape=jax.ShapeDtypeStruct((n, d), F32),
        grid=(n // tm, ff // tf),
        in_specs=[
            pl.BlockSpec((tm, d), lambda i, f: (i, 0)),
            pl.BlockSpec((1, d), lambda i, f: (0, 0)),
            pl.BlockSpec((d, tf), lambda i, f: (0, f)),
            pl.BlockSpec((tf, d), lambda i, f: (f, 0)),
        ],
        out_specs=pl.BlockSpec((tm, d), lambda i, f: (i, 0)),
        scratch_shapes=[pltpu.VMEM((tm, d), BF16), pltpu.VMEM((tm, d), F32)],
        compiler_params=_params(("parallel", "arbitrary")),
        name="mlp",
    )(x2d, g, wu, wd)


def _ple_kernel(x_ref, p_ref, g_ref, wp_ref, wg_ref, o_ref):
    x = x_ref[...]
    e = _rms(_dot(p_ref[...].astype(BF16), wp_ref[...])) * g_ref[...]
    gate = jax.nn.sigmoid(_dot(_rms(x).astype(BF16), wg_ref[...]))
    o_ref[...] = x + gate * e


def _ple(x2d, p2d, g, wp, wg, tm):
    n, d = x2d.shape
    row = lambda i: (i, 0)
    const = lambda i: (0, 0)
    resident = pl.Buffered(1)
    return pl.pallas_call(
        _ple_kernel,
        out_shape=jax.ShapeDtypeStruct((n, d), F32),
        grid=(n // tm,),
        in_specs=[
            pl.BlockSpec((tm, d), row),
            pl.BlockSpec((tm, PLE_DIM), row),
            pl.BlockSpec((1, d), const),
            pl.BlockSpec((PLE_DIM, d), const, pipeline_mode=resident),
            pl.BlockSpec((d, d), const, pipeline_mode=resident),
        ],
        out_specs=pl.BlockSpec((tm, d), row),
        compiler_params=_params(("parallel",)),
        name="ple",
    )(x2d, p2d, g, wp, wg)


def _rope_tables(positions):
    rot = 2 * ROT_HALF
    inv_freq = jnp.power(ROPE_THETA, -jnp.arange(0, rot, 2, dtype=F32) / rot)
    ang = positions.astype(F32)[..., None] * inv_freq
    cos, sin = jnp.cos(ang), jnp.sin(ang)
    n = cos.shape[0] * cos.shape[1]
    cos, sin = cos.reshape(n, ROT_HALF), sin.reshape(n, ROT_HALF)
    rest = NSA_HEAD_DIM - rot
    ones, zeros_r, zeros_h = jnp.ones((n, rest), F32), jnp.zeros((n, rest), F32), jnp.zeros((n, ROT_HALF), F32)
    ra = jnp.concatenate([cos, cos, ones], axis=1)
    rb = jnp.concatenate([-sin, zeros_h, zeros_r], axis=1)
    rc = jnp.concatenate([zeros_h, sin, zeros_r], axis=1)
    return tuple(jnp.tile(t, (1, LANES // NSA_HEAD_DIM)) for t in (ra, rb, rc))


def _regroup_w_in(w_in):
    d = w_in.shape[0]
    ng0 = 3 * DIFF_W + NSA_W + 6 * NSA_KV_W
    per_g = 3 * NSA_GROUP_SIZE
    parts = [w_in[:, :ng0]]
    for g in range(NSA_KV_GROUPS):
        parts += [w_in[:, ng0 + g * per_g:ng0 + (g + 1) * per_g], jnp.zeros((d, LANES - per_g), w_in.dtype)]
    w_main = jnp.concatenate(parts, axis=1).astype(BF16)
    w_gate = w_in[:, ng0 + 3 * NSA_HEADS:].astype(BF16)
    return w_main, w_gate


def kernel(x, p, positions, norm_mix, w_in, diff_q_norm, diff_k_norm, diff_lambda, diff_subln,
           nsa_q_norm, nsa_k_norm, cmp_pos, cmp_w1, cmp_w2, w_proj_diff, w_proj_nsa, w_out,
           norm_mlp, w_mlp_up, w_mlp_down, w_ple_proj, norm_ple, w_ple_gate):
    b, s, d = x.shape
    n = b * s
    tl = _tiles(n)
    assert d == D_MODEL and s % tl.attn == 0
    ra, rb, rc = _rope_tables(positions)
    seg = (jnp.arange(LANES)[:, None] // NSA_HEAD_DIM == jnp.arange(LANES)[None, :] // NSA_HEAD_DIM).astype(BF16)
    ncp = s // CMP_STRIDE
    n_sel = s // SLC_BLOCK
    assert n_sel % 16 == 0 and n_sel <= LANES
    cs = jnp.arange(ncp)[None, :] * CMP_STRIDE
    ss = jnp.arange(n_sel)[:, None] * SLC_BLOCK
    ovlt = ((cs < ss + SLC_BLOCK) & (cs + CMP_BLOCK > ss)).astype(BF16)

    x2d = x.reshape(n, d)
    for i in range(p.shape[0]):
        lambda_init = 0.8 - 0.6 * math.exp(-0.3 * i)
        dup = lambda v: jnp.tile(v, LANES // NSA_HEAD_DIM)
        gains = jnp.zeros((8, LANES), F32).at[0].set(dup(diff_q_norm[i])).at[1].set(dup(diff_k_norm[i]))
        gains = gains.at[2].set(dup(nsa_q_norm[i])).at[3].set(dup(nsa_k_norm[i]))

        w_main, w_gate = _regroup_w_in(w_in[i])
        z = _in_proj(x2d, norm_mix[i][None], w_main, tl.proj_m, tl.proj_n)
        zg = _in_proj(x2d, norm_mix[i][None], w_gate, tl.proj_m, tl.proj_n)
        qdt, kd, vdt, nqpt, nqrt, ks, vst, kw, vwt, xc = _prep(z, ra, rb, rc, gains, seg, tl.prep_m, tl.attn)

        half = CMP_STRIDE * NSA_HEAD_DIM
        cmp_kv, cmp_kvt = _compress(
            xc,
            cmp_pos[i].reshape(2, 2, half),
            cmp_w1[i].reshape(2, 2, half, CMP_HIDDEN).astype(BF16),
            jnp.tile(cmp_w2[i], (1, 1, LANES // NSA_HEAD_DIM)).astype(BF16),
            gains[3:4], b)

        ya = _diff_attn(diff_lambda[i], qdt, kd.reshape(b, s, DIFF_W), vdt, diff_subln[i][None],
                        lambda_init, tl.attn)
        yb = _nsa_attn(nqpt, nqrt, cmp_kv, cmp_kvt, ks, vst, kw, vwt, z, ovlt, b, s, tl.attn)

        x2d = _merge_out(x2d, ya.reshape(n, DIFF_W), yb.reshape(n, NSA_W), zg,
                         w_proj_diff[i].astype(BF16), w_proj_nsa[i].astype(BF16), w_out[i].astype(BF16),
                         tl.merge_m, tl.merge_n)
        x2d = _mlp(x2d, norm_mlp[i][None], w_mlp_up[i].astype(BF16), w_mlp_down[i].astype(BF16),
                   tl.mlp_m, tl.mlp_f)
        x2d = _ple(x2d, p[i].reshape(n, PLE_DIM), norm_ple[i][None], w_ple_proj[i].astype(BF16),
                   w_ple_gate[i].astype(BF16), tl.ple_m)
    return x2d.reshape(b, s, d)
```

```python
import functools
import math
from typing import NamedTuple

import jax
import jax.numpy as jnp
from jax import lax
from jax.experimental import pallas as pl
from jax.experimental.pallas import tpu as pltpu

F32 = jnp.float32
BF16 = jnp.bfloat16

D_MODEL = 2048
PLE_DIM = 256
ROPE_THETA = 500000.0
ROPE_FRACTION = 4
NORM_EPS = 1e-6
NEG_INF = -1e30
DIFF_HEADS = 8
DIFF_SUB_DIM = 64
DIFF_V_DIM = 2 * DIFF_SUB_DIM
NSA_HEADS = 16
NSA_KV_GROUPS = 2
NSA_GROUP_SIZE = NSA_HEADS // NSA_KV_GROUPS
NSA_HEAD_DIM = 64
CMP_BLOCK = 32
CMP_STRIDE = 16
CMP_HIDDEN = 256
SLC_BLOCK = 64
SLC_TOP_N = 16
SLC_FORCED_BONUS = 1e4
WINDOW = 512
DIFF_W = DIFF_HEADS * DIFF_V_DIM
NSA_W = NSA_HEADS * NSA_HEAD_DIM
NSA_KV_W = NSA_KV_GROUPS * NSA_HEAD_DIM
ROT_HALF = NSA_HEAD_DIM // ROPE_FRACTION // 2
QK_SCALE = NSA_HEAD_DIM ** -0.5
Q_SCALE = QK_SCALE * math.log2(math.e)
SUM_ROWS = 16

LANES = 128
VMEM_LIMIT = 56 * 1024 * 1024
NSA_Q_TILE = 2 * LANES

Z_DQ, Z_DK, Z_DV, Z_NQ = 0, 1024, 2048, 3072
Z_KC, Z_VC, Z_KS, Z_VS, Z_KW, Z_VW = 4096, 4224, 4352, 4480, 4608, 4736
PREP_W = 4864
Z_NG = PREP_W


def _params(sem):
    return pltpu.CompilerParams(dimension_semantics=sem, vmem_limit_bytes=VMEM_LIMIT)


class _Tiles(NamedTuple):
    proj_m: int
    proj_n: int
    prep_m: int
    attn: int
    merge_m: int
    merge_n: int
    mlp_m: int
    mlp_f: int
    ple_m: int


def _pick(n, pref):
    t = min(pref, n)
    while n % t:
        t //= 2
    return t


def _tiles(n):
    attn = 512
    return _Tiles(proj_m=_pick(n, 1024), proj_n=1024, prep_m=_pick(attn, 256), attn=attn,
                  merge_m=_pick(n, 256), merge_n=512, mlp_m=_pick(n, 512), mlp_f=1024, ple_m=_pick(n, 512))


def _dot(a, b):
    return jnp.dot(a, b, preferred_element_type=F32)


def _rms(x, eps=NORM_EPS):
    return x * lax.rsqrt(jnp.mean(x * x, axis=-1, keepdims=True) + eps)


def _split_dot(a, m_bf16, terms):
    acc = None
    rem = a
    for t in range(terms):
        piece = rem.astype(BF16)
        d = _dot(piece, m_bf16)
        acc = d if acc is None else acc + d
        if t + 1 < terms:
            rem = rem - piece.astype(F32)
    return acc


def _in_proj_kernel(x_ref, g_ref, wm_ref, wg_ref, z_ref, zg_ref, h_ref, *, main_tiles):
    j = pl.program_id(1)

    @pl.when(j == 0)
    def _():
        h_ref[...] = (_rms(x_ref[...]) * g_ref[...]).astype(BF16)

    @pl.when(j < main_tiles)
    def _():
        z_ref[...] = _dot(h_ref[...], wm_ref[...])

    @pl.when(j >= main_tiles)
    def _():
        zg_ref[...] = _dot(h_ref[...], wg_ref[...])


def _in_proj(x2d, g, w_main, w_gate, tm, tn):
    n, d = x2d.shape
    n_main, n_gate = w_main.shape[1], w_gate.shape[1]
    main_tiles = n_main // tn
    in_main = lambda i, j: (0, jnp.minimum(j, main_tiles - 1))
    in_gate = lambda i, j: (0, jnp.maximum(j - main_tiles, 0))
    return pl.pallas_call(
        functools.partial(_in_proj_kernel, main_tiles=main_tiles),
        out_shape=[jax.ShapeDtypeStruct((n, n_main), F32), jax.ShapeDtypeStruct((n, n_gate), F32)],
        grid=(n // tm, (n_main + n_gate) // tn),
        in_specs=[
            pl.BlockSpec((tm, d), lambda i, j: (i, 0)),
            pl.BlockSpec((1, d), lambda i, j: (0, 0)),
            pl.BlockSpec((d, tn), in_main),
            pl.BlockSpec((d, tn), in_gate),
        ],
        out_specs=[pl.BlockSpec((tm, tn), lambda i, j: (i, jnp.minimum(j, main_tiles - 1))),
                   pl.BlockSpec((tm, tn), lambda i, j: (i, jnp.maximum(j - main_tiles, 0)))],
        scratch_shapes=[pltpu.VMEM((tm, d), BF16)],
        compiler_params=_params(("parallel", "arbitrary")),
        name="in_proj",
    )(x2d, g, w_main, w_gate)


def _prep_kernel(z_ref, ra_ref, rb_ref, rc_ref, gain_ref, seg_ref,
                 qdt_ref, kd_ref, vdt_ref, nqpt_ref, nqrt_ref, ks_ref, vst_ref, kw_ref, vwt_ref, xc_ref,
                 stage_ref):
    ra, rb, rc = ra_ref[...], rb_ref[...], rc_ref[...]
    seg = seg_ref[...]
    tm = ra.shape[0]
    nt = tm // LANES

    def rope(y):
        return (y * ra + pltpu.roll(y, LANES - ROT_HALF, 1) * rb + pltpu.roll(y, ROT_HALF, 1) * rc)

    def head_norm(xb, gain):
        ss = _split_dot(xb * xb, seg, 2)
        return xb * lax.rsqrt(ss * (1.0 / NSA_HEAD_DIM) + NORM_EPS) * gain

    def zblk(col):
        return z_ref[:, col:col + LANES]

    for hb in range(DIFF_W // LANES):
        c = hb * LANES
        qd = rope(head_norm(zblk(Z_DQ + c), gain_ref[0:1, :])) * Q_SCALE
        qdt_ref[c:c + LANES, :] = qd.T.astype(BF16)
        kd_ref[:, c:c + LANES] = rope(head_norm(zblk(Z_DK + c), gain_ref[1:2, :])).astype(BF16)
        vdt_ref[c:c + LANES, :] = zblk(Z_DV + c).T.astype(BF16)
        yq = head_norm(zblk(Z_NQ + c), gain_ref[2:3, :])
        qpt = (yq * Q_SCALE).T.astype(BF16)
        qrt = (rope(yq) * Q_SCALE).T.astype(BF16)
        for j in range(nt):
            nqpt_ref[j, c:c + LANES, :] = qpt[:, j * LANES:(j + 1) * LANES]
            nqrt_ref[j, c:c + LANES, :] = qrt[:, j * LANES:(j + 1) * LANES]

    ks_ref[...] = rope(head_norm(zblk(Z_KS), gain_ref[3:4, :])).astype(BF16)
    kw_ref[...] = rope(head_norm(zblk(Z_KW), gain_ref[3:4, :])).astype(BF16)
    hd = NSA_HEAD_DIM
    for col, v_out in ((Z_VS, vst_ref), (Z_VW, vwt_ref)):
        vt = zblk(col).T.astype(BF16)
        for g in range(NSA_KV_GROUPS):
            for j in range(nt):
                v_out[g, j] = vt[g * hd:(g + 1) * hd, j * LANES:(j + 1) * LANES]

    nchunk = tm // CMP_STRIDE
    lo = lax.broadcasted_iota(jnp.int32, (nchunk, LANES), 1) < hd
    for kv, col in enumerate((Z_KC, Z_VC)):
        stage_ref[...] = zblk(col)
        for t in range(0, CMP_STRIDE, 2):
            a = stage_ref[pl.ds(t, nchunk, stride=CMP_STRIDE), :]
            b = stage_ref[pl.ds(t + 1, nchunk, stride=CMP_STRIDE), :]
            xc_ref[kv, 0, :, t * hd:(t + 2) * hd] = jnp.where(lo, a, pltpu.roll(b, hd, 1))
            xc_ref[kv, 1, :, t * hd:(t + 2) * hd] = jnp.where(lo, pltpu.roll(a, hd, 1), b)


def _prep(z, ra, rb, rc, gains, seg, tm, t_diff):
    n = z.shape[0]
    nt = tm // LANES
    per = t_diff // tm
    row = lambda i: (i, 0)
    tile_t = lambda i: (i // per, 0, i % per)
    diff_t = jax.ShapeDtypeStruct((n // t_diff, DIFF_W, t_diff), BF16)
    nsa_qt = jax.ShapeDtypeStruct((n // LANES, NSA_W, LANES), BF16)
    nsa_vt = jax.ShapeDtypeStruct((NSA_KV_GROUPS, n // LANES, NSA_HEAD_DIM, LANES), BF16)
    diff_t_spec = pl.BlockSpec((None, DIFF_W, tm), tile_t)
    nsa_qt_spec = pl.BlockSpec((nt, NSA_W, LANES), lambda i: (i, 0, 0))
    nsa_vt_spec = pl.BlockSpec((NSA_KV_GROUPS, nt, NSA_HEAD_DIM, LANES), lambda i: (0, i, 0, 0))
    chunk_w = CMP_STRIDE * NSA_HEAD_DIM
    chunks = jax.ShapeDtypeStruct((2, NSA_KV_GROUPS, n // CMP_STRIDE, chunk_w), F32)
    chunks_spec = pl.BlockSpec((2, NSA_KV_GROUPS, tm // CMP_STRIDE, chunk_w), lambda i: (0, 0, i, 0))
    return pl.pallas_call(
        _prep_kernel,
        out_shape=[diff_t, jax.ShapeDtypeStruct((n, DIFF_W), BF16), diff_t, nsa_qt, nsa_qt,
                   jax.ShapeDtypeStruct((n, LANES), BF16), nsa_vt,
                   jax.ShapeDtypeStruct((n, LANES), BF16), nsa_vt, chunks],
        grid=(n // tm,),
        in_specs=[
            pl.BlockSpec((tm, PREP_W), row),
            pl.BlockSpec((tm, LANES), row),
            pl.BlockSpec((tm, LANES), row),
            pl.BlockSpec((tm, LANES), row),
            pl.BlockSpec((8, LANES), lambda i: (0, 0)),
            pl.BlockSpec((LANES, LANES), lambda i: (0, 0)),
        ],
        out_specs=[diff_t_spec, pl.BlockSpec((tm, DIFF_W), row), diff_t_spec, nsa_qt_spec, nsa_qt_spec,
                   pl.BlockSpec((tm, LANES), row), nsa_vt_spec,
                   pl.BlockSpec((tm, LANES), row), nsa_vt_spec, chunks_spec],
        scratch_shapes=[pltpu.VMEM((tm, LANES), F32)],
        compiler_params=_params(("parallel",)),
        name="prep",
    )(z, ra, rb, rc, gains, seg)


def _compress_kernel(x_ref, pos_ref, w1_ref, w2_ref, gain_ref, o_ref, ot_ref):
    x = x_ref[...]
    nc = x.shape[0]
    a = (x + pos_ref[0:1, :]).astype(BF16)
    b = (x + pos_ref[1:2, :]).astype(BF16)
    u = _dot(a, w1_ref[0])
    v = _dot(b, w1_ref[1])
    pre = u + pltpu.roll(v, nc - 1, 0)
    cdf = 0.5 * (1.0 + jnp.tanh(math.sqrt(2.0 / math.pi) * (pre + 0.044715 * (pre * pre * pre))))
    c = _dot((pre * cdf).astype(BF16), w2_ref[...])
    is_key = pl.program_id(0) == 0
    out = jnp.where(is_key, _rms(c) * gain_ref[...], c)
    o_ref[...] = out.astype(BF16)
    ot_ref[...] = out.T.astype(BF16)


def _compress(xc, pos, w1, w2d, gain, b):
    _, g, n_chunks, cw = xc.shape
    nc = n_chunks // b
    return pl.pallas_call(
        _compress_kernel,
        out_shape=[jax.ShapeDtypeStruct((2, b, g, nc, LANES), BF16),
                   jax.ShapeDtypeStruct((2, b, g, LANES, nc), BF16)],
        grid=(2, b, g),
        in_specs=[
            pl.BlockSpec((None, None, nc, cw), lambda t, i, j: (t, j, i, 0)),
            pl.BlockSpec((None, 2, cw), lambda t, i, j: (t, 0, 0)),
            pl.BlockSpec((None, 2, cw, CMP_HIDDEN), lambda t, i, j: (t, 0, 0, 0)),
            pl.BlockSpec((None, CMP_HIDDEN, LANES), lambda t, i, j: (t, 0, 0)),
            pl.BlockSpec((1, LANES), lambda t, i, j: (0, 0)),
        ],
        out_specs=[pl.BlockSpec((None, None, None, nc, LANES), lambda t, i, j: (t, i, j, 0, 0)),
                   pl.BlockSpec((None, None, None, LANES, nc), lambda t, i, j: (t, i, j, 0, 0))],
        compiler_params=_params(("parallel", "parallel", "parallel")),
        name="compress",
    )(xc, pos, w1, w2d, gain)


def _online_update(s, vt, m_ref, acc_ref, tile_max=None):
    m_old = m_ref[...]
    tile_max = jnp.max(s, axis=0, keepdims=True) if tile_max is None else tile_max
    m_new = jnp.maximum(m_old, tile_max)
    p = jnp.exp2(s - m_new)
    acc_ref[...] = jnp.exp2(m_old - m_new) * acc_ref[...] + _dot(vt, p.astype(BF16))
    m_ref[...] = m_new


def _with_sum_rows(vt):
    rows = lax.broadcasted_iota(jnp.int32, (SUM_ROWS, vt.shape[1]), 0)
    return jnp.concatenate([vt, jnp.where(rows == 0, 1.0, 0.0).astype(vt.dtype)], axis=0)


def _normalized(acc, dv):
    return acc[0:dv, :] / acc[dv:dv + 1, :]


def _diff_attn_kernel(lam_ref, qt_ref, k_ref, vt_ref, g_ref, o_ref,
                      m1, a1, m2, a2, sa_ref, sb_ref, mxa_ref, mxb_ref, *, lambda_init, t):
    qi = pl.program_id(2)
    nq = qt_ref.shape[0]

    def sub_heads(tile):
        qt = qt_ref[tile]
        row = lax.broadcasted_iota(jnp.int32, qt.shape, 0)
        zero = jnp.zeros_like(qt)
        return jnp.where(row < DIFF_SUB_DIM, qt, zero), jnp.where(row >= DIFF_SUB_DIM, qt, zero)

    q_sub = sub_heads(qi)
    stats = ((m1, a1), (m2, a2))
    for m_ref, a_ref in stats:
        m_ref[...] = jnp.full(m_ref.shape, NEG_INF, F32)
        a_ref[...] = jnp.zeros(a_ref.shape, F32)

    def scores(kt, buf, q_pair=q_sub):
        s_ref, mx_ref = buf
        k = k_ref[pl.ds(pl.multiple_of(kt * t, t), t), :]
        for i in range(2):
            s = _dot(k, q_pair[i])
            s_ref[i] = s
            mx_ref[i] = jnp.max(s, axis=0, keepdims=True)

    def absorb(kt, buf, causal):
        s_ref, mx_ref = buf
        vt = _with_sum_rows(vt_ref[kt])
        for i in range(2):
            s = s_ref[i]
            if causal:
                kpos = lax.broadcasted_iota(jnp.int32, s.shape, 0)
                qpos = lax.broadcasted_iota(jnp.int32, s.shape, 1)
                _online_update(jnp.where(kpos <= qpos, s, NEG_INF), vt, *stats[i])
            else:
                _online_update(s, vt, *stats[i], mx_ref[i])

    buf_a, buf_b = (sa_ref, mxa_ref), (sb_ref, mxb_ref)

    @pl.when(qi == 0)
    def _():
        scores(0, buf_a)

    def body(j, carry):
        scores(2 * j + 1, buf_b)
        absorb(2 * j, buf_a, False)
        scores(2 * j + 2, buf_a)
        absorb(2 * j + 1, buf_b, False)
        return carry

    lax.fori_loop(0, qi // 2, body, 0)

    @pl.when(qi % 2 == 0)
    def _():
        absorb(qi, buf_a, True)

    @pl.when(qi % 2 == 1)
    def _():
        scores(qi, buf_b)
        absorb(qi - 1, buf_a, False)
        absorb(qi, buf_b, True)

    lp = lam_ref[...]
    lam = (jnp.exp(jnp.sum(lp[0:1] * lp[1:2], axis=-1, keepdims=True))
           - jnp.exp(jnp.sum(lp[2:3] * lp[3:4], axis=-1, keepdims=True)) + lambda_init)
    o = (_normalized(a1[...], DIFF_V_DIM) - lam * _normalized(a2[...], DIFF_V_DIM)).T
    o_ref[...] = (_rms(o) * g_ref[...] * (1.0 - lambda_init)).astype(BF16)

    scores(0, buf_a, sub_heads(jnp.minimum(qi + 1, nq - 1)))


def _diff_attn(lam_p, qdt, kd, vdt, subln, lambda_init, t):
    b, s, _ = kd.shape
    nq = s // t
    kernel = functools.partial(_diff_attn_kernel, lambda_init=lambda_init, t=t)
    dv = DIFF_V_DIM + SUM_ROWS
    stat = pltpu.VMEM((1, t), F32)
    acc = pltpu.VMEM((dv, t), F32)
    score_buf = pltpu.VMEM((2, t, t), F32)
    return pl.pallas_call(
        kernel,
        out_shape=jax.ShapeDtypeStruct((b, s, DIFF_W), BF16),
        grid=(b, DIFF_HEADS, nq),
        in_specs=[
            pl.BlockSpec((4, DIFF_SUB_DIM), lambda i, h, j: (0, 0)),
            pl.BlockSpec((nq, LANES, t), lambda i, h, j: (i, h, 0)),
            pl.BlockSpec((None, s, LANES), lambda i, h, j: (i, 0, h)),
            pl.BlockSpec((nq, LANES, t), lambda i, h, j: (i, h, 0)),
            pl.BlockSpec((1, LANES), lambda i, h, j: (0, 0)),
        ],
        out_specs=pl.BlockSpec((None, t, LANES), lambda i, h, j: (i, j, h)),
        scratch_shapes=[stat, acc, stat, acc, score_buf, score_buf,
                        pltpu.VMEM((2, 1, t), F32), pltpu.VMEM((2, 1, t), F32)],
        compiler_params=_params(("parallel", "parallel", "arbitrary")),
        name="diff_attn",
    )(lam_p, qdt, kd, vdt, subln)


def _nsa_kernel(qpt_ref, qrt_ref, kc_ref, vct_ref, ks_ref, vst_ref, kw_ref, vwt_ref, ng_ref,
                ovlt_ref, o_ref, m_ref, acc_ref, part_ref, bias_ref, sa_ref, sb_ref, mxa_ref, mxb_ref,
                *, tq, tk, n_sel, top_n, wlen):
    r_heads = NSA_GROUP_SIZE
    hd = NSA_HEAD_DIM
    g = pl.program_id(1)
    q0 = pl.program_id(2) * tq
    qpos = q0 + lax.broadcasted_iota(jnp.int32, (1, tq), 1)

    def heads(x):
        return jnp.concatenate([x] * r_heads, axis=1)

    def head_cols(x, r):
        return x[:, r * tq:(r + 1) * tq]

    def stack_q(qt_ref):
        qt = jnp.concatenate([qt_ref[t] for t in range(tq // LANES)], axis=1)
        q64 = jnp.concatenate([qt[r * hd:(r + 1) * hd, :] for r in range(r_heads)], axis=1)
        q128 = jnp.concatenate([q64, q64], axis=0)
        half = lax.broadcasted_iota(jnp.int32, q128.shape, 0) // hd
        return jnp.where(half == g, q128, jnp.zeros_like(q128))

    q_plain = stack_q(qpt_ref)
    q_rot = stack_q(qrt_ref)

    ncp = kc_ref.shape[0]
    start = pl.multiple_of(jnp.maximum(q0 + tq - wlen, 0), LANES)
    raw_c = _dot(kc_ref[...], q_plain)
    raw_w = _dot(kw_ref[pl.ds(start, wlen), :], q_rot)

    n_idx = lax.broadcasted_iota(jnp.int32, (ncp, tq), 0)
    c_ok = (n_idx * CMP_STRIDE + (CMP_BLOCK - 1)) <= qpos
    s_c = raw_c + heads(jnp.where(c_ok, 0.0, NEG_INF))
    e_c = jnp.exp2(s_c - jnp.max(s_c, axis=0, keepdims=True))
    col = lax.broadcasted_iota(jnp.int32, (1, r_heads * tq), 1)
    any_ok = jnp.where(q0 + col % tq >= CMP_BLOCK - 1, 1.0, 0.0)
    p_c = e_c * (any_ok / jnp.sum(e_c, axis=0, keepdims=True))
    o_c = _dot(vct_ref[0:hd, :], p_c.astype(BF16))

    p_sum = head_cols(p_c, 0)
    for r in range(1, r_heads):
        p_sum = p_sum + head_cols(p_c, r)
    ovlt = ovlt_ref[...]
    nb = ovlt.shape[0]
    imp, rem = None, p_sum
    for term in range(3):
        piece = rem.astype(BF16)
        d = _dot(ovlt, piece)
        imp = d if imp is None else imp + d
        rem = rem - piece.astype(F32)
    blk = lax.broadcasted_iota(jnp.int32, (nb, tq), 0)
    q_blk = qpos // SLC_BLOCK
    valid = blk <= q_blk
    forced = (blk == 0) | (blk == q_blk) | (blk == q_blk - 1)
    score = jnp.where(valid, imp + SLC_FORCED_BONUS * jnp.where(forced, 1.0, 0.0), -1.0)
    rank = jnp.zeros((nb, tq), F32)
    for jp in range(n_sel):
        other = score[jp:jp + 1, :]
        gt = jnp.where(other > score, 1.0, 0.0)
        ge = jnp.where(other >= score, 1.0, 0.0)
        rank = rank + jnp.where(blk > jp, ge, gt)
    bias_ref[...] = jnp.where(valid & (rank < float(top_n)), 0.0, NEG_INF)

    vwt = _with_sum_rows(jnp.concatenate([vwt_ref[start // LANES + j] for j in range(wlen // LANES)], axis=1))
    dist = qpos - (start + lax.broadcasted_iota(jnp.int32, (wlen, tq), 0))
    w_ok = (dist >= 0) & (dist < WINDOW)
    s_w = raw_w + heads(jnp.where(w_ok, 0.0, NEG_INF))
    e_w = jnp.exp2(s_w - jnp.max(s_w, axis=0, keepdims=True))
    o_w = _normalized(_dot(vwt, e_w.astype(BF16)), hd)

    gt_all = jax.nn.sigmoid(ng_ref[...]).T

    def gate(r, c):
        return gt_all[3 * r + c:3 * r + c + 1, :]

    for r in range(r_heads):
        part_ref[r * hd:(r + 1) * hd, :] = gate(r, 0) * head_cols(o_c, r) + gate(r, 2) * head_cols(o_w, r)

    m_ref[...] = jnp.full(m_ref.shape, NEG_INF, F32)
    acc_ref[...] = jnp.zeros(acc_ref.shape, F32)
    sub = tk // LANES
    blk_per_tile = tk // SLC_BLOCK

    def scores(kt, buf):
        s_ref, mx_ref = buf
        bias = jnp.concatenate(
            [jnp.broadcast_to(bias_ref[pl.ds(kt * blk_per_tile + i, 1), :], (SLC_BLOCK, tq))
             for i in range(blk_per_tile)], axis=0)
        kpos = kt * tk + lax.broadcasted_iota(jnp.int32, (tk, tq), 0)
        bias = jnp.where(kpos <= qpos, bias, NEG_INF)
        s = _dot(ks_ref[pl.ds(pl.multiple_of(kt * tk, tk), tk), :], q_rot) + heads(bias)
        s_ref[...] = s
        mx_ref[...] = jnp.max(s, axis=0, keepdims=True)

    def absorb(kt, buf):
        s_ref, mx_ref = buf
        vt = _with_sum_rows(jnp.concatenate([vst_ref[kt * sub + j] for j in range(sub)], axis=1))
        _online_update(s_ref[...], vt, m_ref, acc_ref, mx_ref[...])

    kt_last = q0 // tk
    buf_a, buf_b = (sa_ref, mxa_ref), (sb_ref, mxb_ref)
    scores(0, buf_a)

    def sel_body(j, carry):
        scores(2 * j + 1, buf_b)
        absorb(2 * j, buf_a)
        scores(2 * j + 2, buf_a)
        absorb(2 * j + 1, buf_b)
        return carry

    lax.fori_loop(0, kt_last // 2, sel_body, 0)

    @pl.when(kt_last % 2 == 0)
    def _():
        absorb(kt_last, buf_a)

    @pl.when(kt_last % 2 == 1)
    def _():
        scores(kt_last, buf_b)
        absorb(kt_last - 1, buf_a)
        absorb(kt_last, buf_b)

    o_s = _normalized(acc_ref[...], hd)
    outs = [part_ref[r * hd:(r + 1) * hd, :] + gate(r, 1) * head_cols(o_s, r) for r in range(r_heads)]
    o_ref[...] = jnp.concatenate(outs, axis=0).T.astype(BF16)


def _nsa_attn(nqpt, nqrt, cmp_kv, cmp_kvt, ks, vst, kw, vwt, z, ovlt, b, s, tk):
    tq = NSA_Q_TILE
    assert tk % tq == 0
    n_sel = s // SLC_BLOCK
    top_n = min(SLC_TOP_N, n_sel)
    wlen = min(WINDOW + tq, s)
    ncp = cmp_kv.shape[3]
    gw = NSA_GROUP_SIZE * NSA_HEAD_DIM
    nq = s // tq
    kernel = functools.partial(_nsa_kernel, tq=tq, tk=tk, n_sel=n_sel, top_n=top_n, wlen=wlen)
    qt_spec = pl.BlockSpec((tq // LANES, gw, LANES), lambda i, g, j: (i * nq + j, g, 0))
    k_spec = pl.BlockSpec((s, LANES), lambda i, g, j: (i, 0))
    dv = NSA_HEAD_DIM + SUM_ROWS
    vt_spec = pl.BlockSpec((None, s // LANES, NSA_HEAD_DIM, LANES), lambda i, g, j: (g, i, 0, 0))
    cols = NSA_GROUP_SIZE * tq
    return pl.pallas_call(
        kernel,
        out_shape=jax.ShapeDtypeStruct((b, s, NSA_W), BF16),
        grid=(b, NSA_KV_GROUPS, nq),
        in_specs=[
            qt_spec, qt_spec,
            pl.BlockSpec((None, None, None, ncp, LANES), lambda i, g, j: (0, i, g, 0, 0)),
            pl.BlockSpec((None, None, None, LANES, ncp), lambda i, g, j: (1, i, g, 0, 0)),
            k_spec, vt_spec, k_spec, vt_spec,
            pl.BlockSpec((tq, LANES), lambda i, g, j: (i * nq + j, Z_NG // LANES + g)),
            pl.BlockSpec(ovlt.shape, lambda i, g, j: (0, 0)),
        ],
        out_specs=pl.BlockSpec((None, tq, gw), lambda i, g, j: (i, j, g)),
        scratch_shapes=[pltpu.VMEM((1, cols), F32),
                        pltpu.VMEM((dv, cols), F32), pltpu.VMEM((gw, tq), F32),
                        pltpu.VMEM((n_sel, tq), F32),
                        pltpu.VMEM((tk, cols), F32), pltpu.VMEM((tk, cols), F32),
                        pltpu.VMEM((1, cols), F32), pltpu.VMEM((1, cols), F32)],
        compiler_params=_params(("parallel", "parallel", "arbitrary")),
        name="nsa_attn",
    )(nqpt, nqrt, cmp_kv, cmp_kvt, ks, vst, kw, vwt, z, ovlt)


def _merge_out_kernel(x_ref, ya_ref, yb_ref, ga_ref, gb_ref, wa_ref, wb_ref, wo_ref, o_ref, mg_ref,
                      *, tn):
    d = x_ref.shape[1]
    ya = ya_ref[...]
    yb = yb_ref[...]
    for c in range(0, d, tn):
        ta = _dot(ya, wa_ref[:, c:c + tn])
        tb = _dot(yb, wb_ref[:, c:c + tn])
        mg_ref[:, c:c + tn] = (jax.nn.sigmoid(ga_ref[:, c:c + tn]) * ta
                               + jax.nn.sigmoid(gb_ref[:, c:c + tn]) * tb).astype(BF16)
    mg = mg_ref[...]
    for c in range(0, d, tn):
        o_ref[:, c:c + tn] = x_ref[:, c:c + tn] + _dot(mg, wo_ref[:, c:c + tn])


def _merge_out(x2d, ya, yb, zg, wa, wb, wo, tm, tn):
    n, d = x2d.shape
    row = lambda i: (i, 0)
    const = lambda i: (0, 0)
    resident = pl.Buffered(1)
    return pl.pallas_call(
        functools.partial(_merge_out_kernel, tn=tn),
        out_shape=jax.ShapeDtypeStruct((n, d), F32),
        grid=(n // tm,),
        in_specs=[
            pl.BlockSpec((tm, d), row),
            pl.BlockSpec((tm, DIFF_W), row),
            pl.BlockSpec((tm, NSA_W), row),
            pl.BlockSpec((tm, d), lambda i: (i, 0)),
            pl.BlockSpec((tm, d), lambda i: (i, 1)),
            pl.BlockSpec((DIFF_W, d), const, pipeline_mode=resident),
            pl.BlockSpec((NSA_W, d), const, pipeline_mode=resident),
            pl.BlockSpec((d, d), const, pipeline_mode=resident),
        ],
        out_specs=pl.BlockSpec((tm, d), row),
        scratch_shapes=[pltpu.VMEM((tm, d), BF16)],
        compiler_params=_params(("parallel",)),
        name="merge_out",
    )(x2d, ya, yb, zg, zg, wa, wb, wo)


def _mlp_kernel(x_ref, g_ref, wu_ref, wd_ref, o_ref, h_ref, acc_ref):
    f = pl.program_id(1)

    @pl.when(f == 0)
    def _():
        h_ref[...] = (_rms(x_ref[...]) * g_ref[...]).astype(BF16)
        acc_ref[...] = jnp.zeros(acc_ref.shape, F32)

    u = jnp.maximum(_dot(h_ref[...], wu_ref[...]), 0.0)
    acc_ref[...] += _dot((u * u).astype(BF16), wd_ref[...])

    @pl.when(f == pl.num_programs(1) - 1)
    def _():
        o_ref[...] = x_ref[...] + acc_ref[...]


def _mlp(x2d, g, wu, wd, tm, tf):
    n, d = x2d.shape
    ff = wu.shape[1]
    return pl.pallas_call(
        _mlp_kernel,
        out_shape=jax.ShapeDtypeStruct((n, d), F32),
        grid=(n // tm, ff // tf),
        in_specs=[
            pl.BlockSpec((tm, d), lambda i, f: (i, 0)),
            pl.BlockSpec((1, d), lambda i, f: (0, 0)),
            pl.BlockSpec((d, tf), lambda i, f: (0, f)),
            pl.BlockSpec((tf, d), lambda i, f: (f, 0)),
        ],
        out_specs=pl.BlockSpec((tm, d), lambda i, f: (i, 0)),
        scratch_shapes=[pltpu.VMEM((tm, d), BF16), pltpu.VMEM((tm, d), F32)],
        compiler_params=_params(("parallel", "arbitrary")),
        name="mlp",
    )(x2d, g, wu, wd)


def _ple_kernel(x_ref, p_ref, g_ref, wp_ref, wg_ref, o_ref):
    x = x_ref[...]
    e = _rms(_dot(p_ref[...].astype(BF16), wp_ref[...])) * g_ref[...]
    gate = jax.nn.sigmoid(_dot(_rms(x).astype(BF16), wg_ref[...]))
    o_ref[...] = x + gate * e


def _ple(x2d, p2d, g, wp, wg, tm):
    n, d = x2d.shape
    row = lambda i: (i, 0)
    const = lambda i: (0, 0)
    resident = pl.Buffered(1)
    return pl.pallas_call(
        _ple_kernel,
        out_shape=jax.ShapeDtypeStruct((n, d), F32),
        grid=(n // tm,),
        in_specs=[
            pl.BlockSpec((tm, d), row),
            pl.BlockSpec((tm, PLE_DIM), row),
            pl.BlockSpec((1, d), const),
            pl.BlockSpec((PLE_DIM, d), const, pipeline_mode=resident),
            pl.BlockSpec((d, d), const, pipeline_mode=resident),
        ],
        out_specs=pl.BlockSpec((tm, d), row),
        compiler_params=_params(("parallel",)),
        name="ple",
    )(x2d, p2d, g, wp, wg)


def _rope_tables(positions):
    rot = 2 * ROT_HALF
    inv_freq = jnp.power(ROPE_THETA, -jnp.arange(0, rot, 2, dtype=F32) / rot)
    ang = positions.astype(F32)[..., None] * inv_freq
    cos, sin = jnp.cos(ang), jnp.sin(ang)
    n = cos.shape[0] * cos.shape[1]
    cos, sin = cos.reshape(n, ROT_HALF), sin.reshape(n, ROT_HALF)
    rest = NSA_HEAD_DIM - rot
    ones, zeros_r, zeros_h = jnp.ones((n, rest), F32), jnp.zeros((n, rest), F32), jnp.zeros((n, ROT_HALF), F32)
    ra = jnp.concatenate([cos, cos, ones], axis=1)
    rb = jnp.concatenate([-sin, zeros_h, zeros_r], axis=1)
    rc = jnp.concatenate([zeros_h, sin, zeros_r], axis=1)
    return tuple(jnp.tile(t, (1, LANES // NSA_HEAD_DIM)) for t in (ra, rb, rc))


def _regroup_w_in(w_in):
    d = w_in.shape[0]
    ng0 = 3 * DIFF_W + NSA_W + 6 * NSA_KV_W
    per_g = 3 * NSA_GROUP_SIZE
    parts = [w_in[:, :ng0]]
    for g in range(NSA_KV_GROUPS):
        parts += [w_in[:, ng0 + g * per_g:ng0 + (g + 1) * per_g], jnp.zeros((d, LANES - per_g), w_in.dtype)]
    w_main = jnp.concatenate(parts, axis=1).astype(BF16)
    w_gate = w_in[:, ng0 + 3 * NSA_HEADS:].astype(BF16)
    return w_main, w_gate


def kernel(x, p, positions, norm_mix, w_in, diff_q_norm, diff_k_norm, diff_lambda, diff_subln,
           nsa_q_norm, nsa_k_norm, cmp_pos, cmp_w1, cmp_w2, w_proj_diff, w_proj_nsa, w_out,
           norm_mlp, w_mlp_up, w_mlp_down, w_ple_proj, norm_ple, w_ple_gate):
    b, s, d = x.shape
    n = b * s
    tl = _tiles(n)
    assert d == D_MODEL and s % tl.attn == 0
    ra, rb, rc = _rope_tables(positions)
    seg = (jnp.arange(LANES)[:, None] // NSA_HEAD_DIM == jnp.arange(LANES)[None, :] // NSA_HEAD_DIM).astype(BF16)
    ncp = s // CMP_STRIDE
    n_sel = s // SLC_BLOCK
    assert n_sel % 16 == 0 and n_sel <= LANES
    cs = jnp.arange(ncp)[None, :] * CMP_STRIDE
    ss = jnp.arange(n_sel)[:, None] * SLC_BLOCK
    ovlt = ((cs < ss + SLC_BLOCK) & (cs + CMP_BLOCK > ss)).astype(BF16)

    x2d = x.reshape(n, d)
    for i in range(p.shape[0]):
        lambda_init = 0.8 - 0.6 * math.exp(-0.3 * i)
        dup = lambda v: jnp.tile(v, LANES // NSA_HEAD_DIM)
        gains = jnp.zeros((8, LANES), F32).at[0].set(dup(diff_q_norm[i])).at[1].set(dup(diff_k_norm[i]))
        gains = gains.at[2].set(dup(nsa_q_norm[i])).at[3].set(dup(nsa_k_norm[i]))

        w_main, w_gate = _regroup_w_in(w_in[i])
        z, zg = _in_proj(x2d, norm_mix[i][None], w_main, w_gate, tl.proj_m, tl.proj_n)
        qdt, kd, vdt, nqpt, nqrt, ks, vst, kw, vwt, xc = _prep(z, ra, rb, rc, gains, seg, tl.prep_m, tl.attn)

        half = CMP_STRIDE * NSA_HEAD_DIM
        cmp_kv, cmp_kvt = _compress(
            xc,
            cmp_pos[i].reshape(2, 2, half),
            cmp_w1[i].reshape(2, 2, half, CMP_HIDDEN).astype(BF16),
            jnp.tile(cmp_w2[i], (1, 1, LANES // NSA_HEAD_DIM)).astype(BF16),
            gains[3:4], b)

        ya = _diff_attn(diff_lambda[i], qdt, kd.reshape(b, s, DIFF_W), vdt, diff_subln[i][None],
                        lambda_init, tl.attn)
        yb = _nsa_attn(nqpt, nqrt, cmp_kv, cmp_kvt, ks, vst, kw, vwt, z, ovlt, b, s, tl.attn)

        x2d = _merge_out(x2d, ya.reshape(n, DIFF_W), yb.reshape(n, NSA_W), zg,
                         w_proj_diff[i].astype(BF16), w_proj_nsa[i].astype(BF16), w_out[i].astype(BF16),
                         tl.merge_m, tl.merge_n)
        x2d = _mlp(x2d, norm_mlp[i][None], w_mlp_up[i].astype(BF16), w_mlp_down[i].astype(BF16),
                   tl.mlp_m, tl.mlp_f)
        x2d = _ple(x2d, p[i].reshape(n, PLE_DIM), norm_ple[i][None], w_ple_proj[i].astype(BF16),
                   w_ple_gate[i].astype(BF16), tl.ple_m)
    return x2d.reshape(b, s, d)
```
